```python
import math, functools
import jax, jax.numpy as jnp
from jax import lax
import numpy as np

D_MODEL = 2048
BATCH = 1
SEQ = 8192
DEPTH = 2
DEC_BATCH = 32
DEC_SEQ = 8
PAST_LEN = 8192
PAGE_SIZE = 128

DH_A = 64
DK_A = 2 * DH_A
DV_A = 2 * DH_A
W_A = D_MODEL // 2
H_A = W_A // DV_A
DH_B = 64
W_B = D_MODEL // 4
H_B = W_B // DH_B
DECAY_LORA = 96
AAA_LORA = 96
GATE_LORA = 256
SHIFT_W = 3 * W_B + DECAY_LORA + AAA_LORA + GATE_LORA
W_C = D_MODEL // 4
CONV_K = 31
IN_W = 3 * W_A + SHIFT_W + 2 * W_C
D_FF = 4 * D_MODEL
NUM_BUCKETS = 32
MAX_DISTANCE = 128
Q_BLOCK = 128
RMS_EPS = 1e-6
LN_EPS = 1e-5
LNX_EPS = 64e-5

kernel_name = "hybrid_diffattn_rwkv7_conformer_step"


def rms_norm(x, g, eps=RMS_EPS):
    xf = x.astype(jnp.float32)
    y = xf * lax.rsqrt(jnp.mean(xf * xf, axis=-1, keepdims=True) + eps) * g.astype(jnp.float32)
    return y.astype(x.dtype)


def layer_norm(x, g, b, eps=LN_EPS):
    mu = jnp.mean(x, axis=-1, keepdims=True)
    xc = x - mu
    var = jnp.mean(xc * xc, axis=-1, keepdims=True)
    return xc * lax.rsqrt(var + eps) * g.astype(jnp.float32) + b.astype(jnp.float32)


def lambda_init(l):
    return 0.8 - 0.6 * math.exp(-0.3 * l)


def t5_bucket(rel):
    n = jnp.maximum(-rel, 0)
    max_exact = NUM_BUCKETS // 2
    nf = jnp.maximum(n, 1).astype(jnp.float32)
    large = max_exact + (jnp.log(nf / max_exact) / math.log(MAX_DISTANCE / max_exact)
                         * (NUM_BUCKETS - max_exact)).astype(jnp.int32)
    large = jnp.minimum(large, NUM_BUCKETS - 1)
    return jnp.where(n < max_exact, n, large)


def diff_attention(q, k, v, q_pos, k_pos, lam, rel_bias):
    logits = jnp.einsum('bqhcd,bkhcd->bhcqk', q.astype(jnp.float32), k.astype(jnp.float32)) * (DH_A ** -0.5)
    rel = k_pos[None, :] - q_pos[:, None]
    bias = jnp.transpose(rel_bias.astype(jnp.float32)[t5_bucket(rel)], (2, 0, 1))
    logits = logits + bias[None, :, None]
    logits = jnp.where((rel <= 0)[None, None, None], logits, -jnp.inf)
    p = jax.nn.softmax(logits, axis=-1)
    a = p[:, :, 0] - lam * p[:, :, 1]
    return jnp.einsum('bhqk,bkhd->bqhd', a, v.astype(jnp.float32))


def attn_prompt(q, k, v, lam, rel_bias):
    B, T = q.shape[0], q.shape[1]
    nb = T // Q_BLOCK
    pos = jnp.arange(T, dtype=jnp.int32)
    qb = jnp.moveaxis(q.reshape(B, nb, Q_BLOCK, H_A, 2, DH_A), 1, 0)
    pb = pos.reshape(nb, Q_BLOCK)

    def block(args):
        q_blk, p_blk = args
        return diff_attention(q_blk, k, v, p_blk, pos, lam, rel_bias)

    out = lax.map(block, (qb, pb))
    return jnp.moveaxis(out, 0, 1).reshape(B, T, H_A, DV_A)


def attn_with_past(q, k, v, lam, k_past, v_past, rel_bias):
    T = q.shape[1]
    n_past = k_past.shape[1]
    k_all = jnp.concatenate([k_past.astype(k.dtype), k], axis=1)
    v_all = jnp.concatenate([v_past.astype(v.dtype), v], axis=1)
    k_pos = jnp.arange(n_past + T, dtype=jnp.int32)
    q_pos = n_past + jnp.arange(T, dtype=jnp.int32)
    return diff_attention(q, k_all, v_all, q_pos, k_pos, lam, rel_bias)


def rwkv7_scan(S0, r, w, k, v, a, b):
    def step(S, inp):
        r_t, w_t, k_t, v_t, a_t, b_t = inp
        sa = jnp.einsum('bhvk,bhk->bhv', S, a_t)
        S = S * w_t[:, :, None, :] + sa[..., None] * b_t[:, :, None, :] + v_t[..., None] * k_t[:, :, None, :]
        y = jnp.einsum('bhvk,bhk->bhv', S, r_t)
        return S, y

    xs = tuple(jnp.moveaxis(t, 1, 0) for t in (r, w, k, v, a, b))
    S, ys = lax.scan(step, S0, xs)
    return S, jnp.moveaxis(ys, 0, 1)


def rwkv7_mix(pB, shift0, S0, mu, w0, w2, a0, a2, g2, k_k, k_a, r_k, lnx_g, lnx_b):
    B, T, _ = pB.shape
    f32 = jnp.float32
    pf = pB.astype(f32)
    prev = jnp.concatenate([shift0[:, None].astype(f32), pf[:, :-1]], axis=1)
    xs = pf + (prev - pf) * mu.astype(f32)
    s1, s2, s3 = W_B, 2 * W_B, 3 * W_B
    r, k, v, wd, ad, gd = jnp.split(xs, [s1, s2, s3, s3 + DECAY_LORA, s3 + DECAY_LORA + AAA_LORA], axis=-1)
    w_log = -jax.nn.softplus(-(w0.astype(f32) + jnp.tanh(wd) @ w2.astype(f32))) - 0.5
    decay = jnp.exp(-jnp.exp(w_log))
    a = jax.nn.sigmoid(a0.astype(f32) + ad @ a2.astype(f32))
    g = jax.nn.sigmoid(gd) @ g2.astype(f32)
    heads = lambda t: t.reshape(B, T, H_B, DH_B)
    kk = heads(k * k_k.astype(f32))
    kk = kk / jnp.maximum(jnp.sqrt(jnp.sum(kk * kk, axis=-1, keepdims=True)), 1e-12)
    k = k * (1.0 + (a - 1.0) * k_a.astype(f32))
    r_h, k_h, v_h, a_h = heads(r), heads(k), heads(v), heads(a)
    S, y = rwkv7_scan(S0.astype(f32), r_h, heads(decay), k_h, v_h, -kk, kk * a_h)
    mean = jnp.mean(y, axis=-1, keepdims=True)
    yc = y - mean
    y = yc * lax.rsqrt(jnp.mean(yc * yc, axis=-1, keepdims=True) + LNX_EPS)
    y = y.reshape(B, T, W_B) * lnx_g.astype(f32) + lnx_b.astype(f32)
    bonus = jnp.sum(r_h * k_h * r_k.astype(f32), axis=-1, keepdims=True) * v_h
    y = (y + bonus.reshape(B, T, W_B)) * g
    return y, pB[:, -1], S


def conformer_conv(pC, conv0, dw_w, dw_b, ln_g, ln_b):
    f32 = jnp.float32
    pf = pC.astype(f32)
    u = pf[..., :W_C] * jax.nn.sigmoid(pf[..., W_C:])
    buf = jnp.concatenate([conv0.astype(f32), u], axis=1)
    c = lax.conv_general_dilated(buf, dw_w.astype(f32)[:, None, :], (1,), 'VALID',
                                 dimension_numbers=('NWC', 'WIO', 'NWC'),
                                 feature_group_count=W_C) + dw_b.astype(f32)
    c = layer_norm(c, ln_g, ln_b)
    return jax.nn.silu(c), buf[:, -(CONV_K - 1):]


def layer_forward(x, l, P, attend, shift0, S0, conv0):
    B, T, _ = x.shape
    h = rms_norm(x, P['norm_mix_g'][l])
    p = h @ P['w_in'][l]
    pA, pB, pC = jnp.split(p, [3 * W_A, 3 * W_A + SHIFT_W], axis=-1)
    q, k, v = jnp.split(pA, 3, axis=-1)
    q = rms_norm(q.reshape(B, T, H_A, 2, DH_A), P['q_norm_g'][l])
    k = rms_norm(k.reshape(B, T, H_A, 2, DH_A), P['k_norm_g'][l])
    v = v.reshape(B, T, H_A, DV_A)
    lam_init = lambda_init(l)
    f32 = jnp.float32
    lam = (jnp.exp(jnp.sum(P['lambda_q1'][l].astype(f32) * P['lambda_k1'][l].astype(f32)))
           - jnp.exp(jnp.sum(P['lambda_q2'][l].astype(f32) * P['lambda_k2'][l].astype(f32))) + lam_init)
    o = attend(q, k, v, lam)
    yA = (rms_norm(o, P['subln_g'][l]) * (1.0 - lam_init)).reshape(B, T, W_A)
    yB, shift_new, S_new = rwkv7_mix(pB, shift0, S0, P['rwkv_mu'][l], P['rwkv_w0'][l], P['rwkv_w2'][l],
                                     P['rwkv_a0'][l], P['rwkv_a2'][l], P['rwkv_g2'][l], P['rwkv_k_k'][l],
                                     P['rwkv_k_a'][l], P['rwkv_r_k'][l], P['rwkv_lnx_g'][l], P['rwkv_lnx_b'][l])
    yC, conv_new = conformer_conv(pC, conv0, P['conv_dw_w'][l], P['conv_dw_b'][l],
                                  P['conv_ln_g'][l], P['conv_ln_b'][l])
    y = jnp.concatenate([yA, yB, yC], axis=-1).astype(x.dtype)
    x = x + y @ P['w_out'][l]
    h = rms_norm(x, P['norm_ffn_g'][l])
    x = x + jnp.square(jax.nn.relu(h @ P['w_up'][l])) @ P['w_down'][l]
    return x, k.reshape(B, T, H_A, DK_A), v, S_new, shift_new, conv_new


def setup_inputs(seed: int = 0) -> dict:
    key = jax.random.key(seed)
    ks = jax.random.split(key, 40)
    n = lambda i, shape: jax.random.normal(ks[i], shape, dtype=jnp.float32)
    n_pages = PAST_LEN // PAGE_SIZE
    used = DEC_BATCH * n_pages
    n_pool = used + max(1, used // 4)
    page_table = jax.random.permutation(ks[7], n_pool)[:used].reshape(DEC_BATCH, n_pages).astype(jnp.int32)
    return {
        'x_prompt': n(0, (BATCH, SEQ, D_MODEL)),
        'x_sample': n(1, (DEC_BATCH, DEC_SEQ, D_MODEL)),
        'cache_k': n(2, (DEPTH, n_pool, PAGE_SIZE, H_A, DK_A)),
        'cache_v': n(3, (DEPTH, n_pool, PAGE_SIZE, H_A, DV_A)),
        'state_rwkv': 0.3 * n(4, (DEPTH, DEC_BATCH, H_B, DH_B, DH_B)),
        'state_shift': n(5, (DEPTH, DEC_BATCH, SHIFT_W)),
        'state_conv': 0.5 * n(6, (DEPTH, DEC_BATCH, CONV_K - 1, W_C)),
        'page_table': page_table,
        'rel_bias': 0.5 * n(8, (NUM_BUCKETS, H_A)),
        'norm_mix_g': 1.0 + 0.02 * n(9, (DEPTH, D_MODEL)),
        'w_in': n(10, (DEPTH, D_MODEL, IN_W)) * D_MODEL ** -0.5,
        'q_norm_g': 1.0 + 0.02 * n(11, (DEPTH, DH_A)),
        'k_norm_g': 1.0 + 0.02 * n(12, (DEPTH, DH_A)),
        'lambda_q1': 0.1 * n(13, (DEPTH, DH_A)),
        'lambda_k1': 0.1 * n(14, (DEPTH, DH_A)),
        'lambda_q2': 0.1 * n(15, (DEPTH, DH_A)),
        'lambda_k2': 0.1 * n(16, (DEPTH, DH_A)),
        'subln_g': 1.0 + 0.02 * n(17, (DEPTH, DV_A)),
        'rwkv_mu': jax.random.uniform(ks[18], (DEPTH, SHIFT_W), dtype=jnp.float32),
        'rwkv_w0': 0.5 * n(19, (DEPTH, W_B)),
        'rwkv_w2': 0.5 * n(20, (DEPTH, DECAY_LORA, W_B)) * DECAY_LORA ** -0.5,
        'rwkv_a0': 0.5 * n(21, (DEPTH, W_B)),
        'rwkv_a2': 0.5 * n(22, (DEPTH, AAA_LORA, W_B)) * AAA_LORA ** -0.5,
        'rwkv_g2': n(23, (DEPTH, GATE_LORA, W_B)) * GATE_LORA ** -0.5,
        'rwkv_k_k': 1.0 + 0.05 * n(24, (DEPTH, W_B)),
        'rwkv_k_a': 1.0 + 0.05 * n(25, (DEPTH, W_B)),
        'rwkv_r_k': 0.1 * n(26, (DEPTH, H_B, DH_B)),
        'rwkv_lnx_g': 1.0 + 0.02 * n(27, (DEPTH, W_B)),
        'rwkv_lnx_b': 0.02 * n(28, (DEPTH, W_B)),
        'conv_dw_w': n(29, (DEPTH, CONV_K, W_C)) * CONV_K ** -0.5,
        'conv_dw_b': 0.02 * n(30, (DEPTH, W_C)),
        'conv_ln_g': 1.0 + 0.02 * n(31, (DEPTH, W_C)),
        'conv_ln_b': 0.02 * n(32, (DEPTH, W_C)),
        'w_out': n(33, (DEPTH, D_MODEL, D_MODEL)) * D_MODEL ** -0.5,
        'norm_ffn_g': 1.0 + 0.02 * n(34, (DEPTH, D_MODEL)),
        'w_up': n(35, (DEPTH, D_MODEL, D_FF)) * D_MODEL ** -0.5,
        'w_down': n(36, (DEPTH, D_FF, D_MODEL)) * D_FF ** -0.5,
    }


def reference(x_prompt, x_sample, cache_k, cache_v, state_rwkv, state_shift, state_conv, page_table,
              rel_bias, norm_mix_g, w_in, q_norm_g, k_norm_g, lambda_q1, lambda_k1, lambda_q2, lambda_k2,
              subln_g, rwkv_mu, rwkv_w0, rwkv_w2, rwkv_a0, rwkv_a2, rwkv_g2, rwkv_k_k, rwkv_k_a, rwkv_r_k,
              rwkv_lnx_g, rwkv_lnx_b, conv_dw_w, conv_dw_b, conv_ln_g, conv_ln_b, w_out, norm_ffn_g,
              w_up, w_down):
    P = dict(norm_mix_g=norm_mix_g, w_in=w_in, q_norm_g=q_norm_g, k_norm_g=k_norm_g,
             lambda_q1=lambda_q1, lambda_k1=lambda_k1, lambda_q2=lambda_q2, lambda_k2=lambda_k2,
             subln_g=subln_g, rwkv_mu=rwkv_mu, rwkv_w0=rwkv_w0, rwkv_w2=rwkv_w2, rwkv_a0=rwkv_a0,
             rwkv_a2=rwkv_a2, rwkv_g2=rwkv_g2, rwkv_k_k=rwkv_k_k, rwkv_k_a=rwkv_k_a, rwkv_r_k=rwkv_r_k,
             rwkv_lnx_g=rwkv_lnx_g, rwkv_lnx_b=rwkv_lnx_b, conv_dw_w=conv_dw_w, conv_dw_b=conv_dw_b,
             conv_ln_g=conv_ln_g, conv_ln_b=conv_ln_b, w_out=w_out, norm_ffn_g=norm_ffn_g,
             w_up=w_up, w_down=w_down)

    Bp = x_prompt.shape[0]
    attend_p = functools.partial(attn_prompt, rel_bias=rel_bias)
    yp = x_prompt
    kp_l, vp_l, Sp_l, shp_l, cvp_l = [], [], [], [], []
    for l in range(DEPTH):
        shift0 = jnp.zeros((Bp, SHIFT_W), x_prompt.dtype)
        S0 = jnp.zeros((Bp, H_B, DH_B, DH_B), jnp.float32)
        conv0 = jnp.zeros((Bp, CONV_K - 1, W_C), x_prompt.dtype)
        yp, kr, vr, S, sh, cv = layer_forward(yp, l, P, attend_p, shift0, S0, conv0)
        kp_l.append(kr); vp_l.append(vr); Sp_l.append(S); shp_l.append(sh); cvp_l.append(cv)

    Bd = x_sample.shape[0]
    n_past = page_table.shape[1] * PAGE_SIZE
    ys = x_sample
    ks_l, vs_l, Ss_l, shs_l, cvs_l = [], [], [], [], []
    for l in range(DEPTH):
        k_past = cache_k[l, page_table].reshape(Bd, n_past, H_A, 2, DH_A)
        v_past = cache_v[l, page_table].reshape(Bd, n_past, H_A, DV_A)
        attend_s = functools.partial(attn_with_past, k_past=k_past, v_past=v_past, rel_bias=rel_bias)
        ys, kr, vr, S, sh, cv = layer_forward(ys, l, P, attend_s, state_shift[l], state_rwkv[l], state_conv[l])
        ks_l.append(kr); vs_l.append(vr); Ss_l.append(S); shs_l.append(sh); cvs_l.append(cv)

    k_prompt, v_prompt = jnp.stack(kp_l), jnp.stack(vp_l)
    k_sample, v_sample = jnp.stack(ks_l), jnp.stack(vs_l)
    rwkv_prompt, rwkv_sample = jnp.stack(Sp_l), jnp.stack(Ss_l)
    shift_prompt, shift_sample = jnp.stack(shp_l), jnp.stack(shs_l)
    conv_prompt, conv_sample = jnp.stack(cvp_l), jnp.stack(cvs_l)
    return (yp, ys, k_prompt, v_prompt, k_sample, v_sample, rwkv_prompt, rwkv_sample,
            shift_prompt, shift_sample, conv_prompt, conv_sample)
```

```python
import functools
import math

import jax
import jax.numpy as jnp
from jax import lax
from jax.experimental import pallas as pl
from jax.experimental.pallas import tpu as pltpu

F32 = jnp.float32
BF16 = jnp.bfloat16

DH_A = 64
DV_A = 2 * DH_A
DH_B = 64
DECAY_LORA = 96
AAA_LORA = 96
GATE_LORA = 256
LORA_W = 512
CONV_K = 31
NUM_BUCKETS = 32
MAX_DISTANCE = 128
RMS_EPS = 1e-6
LN_EPS = 1e-5
LNX_EPS = 64e-5
NEG = -1e30

LANES = 128
SUBLANES = 8
VMEM_LIMIT = 56 * 1024 * 1024

ATT_BLK = 256
RW_CHUNK = 64


def _cparams(sem):
    return pltpu.CompilerParams(dimension_semantics=sem, vmem_limit_bytes=VMEM_LIMIT)


def _dot(a, b):
    return jnp.dot(a, b, preferred_element_type=F32)


def _dot_hi(a, b, dims=(((1,), (0,)), ((), ()))):
    return lax.dot_general(a, b, dims, preferred_element_type=F32, precision=lax.Precision.HIGHEST)


_NT = (((1,), (1,)), ((), ()))
_TN = (((0,), (0,)), ((), ()))


def _group_ones(width, group):
    r = lax.broadcasted_iota(jnp.int32, (width, width), 0) // group
    c = lax.broadcasted_iota(jnp.int32, (width, width), 1) // group
    return (r == c).astype(BF16)


def _group_sum(x, ones):
    hi = x.astype(BF16)
    lo = (x - hi.astype(F32)).astype(BF16)
    return _dot(hi, ones) + _dot(lo, ones)


def _sigmoid(x):
    return 1.0 / (1.0 + jnp.exp(-x))


def _inproj_kernel(x_ref, g_ref, w_ref, o_ref, h_ref):
    @pl.when(pl.program_id(1) == 0)
    def _():
        x = x_ref[...]
        ms = jnp.mean(x * x, axis=-1, keepdims=True)
        h_ref[...] = (x * lax.rsqrt(ms + RMS_EPS) * g_ref[...]).astype(BF16)

    o_ref[...] = _dot(h_ref[...], w_ref[...])


def _inproj(x, g, w, bm, bn):
    n, d = x.shape
    wn = w.shape[1]
    return pl.pallas_call(
        _inproj_kernel,
        grid=(n // bm, wn // bn),
        in_specs=[
            pl.BlockSpec((bm, d), lambda i, j: (i, 0)),
            pl.BlockSpec((1, d), lambda i, j: (0, 0)),
            pl.BlockSpec((d, bn), lambda i, j: (0, j)),
        ],
        out_specs=pl.BlockSpec((bm, bn), lambda i, j: (i, j)),
        out_shape=jax.ShapeDtypeStruct((n, wn), F32),
        scratch_shapes=[pltpu.VMEM((bm, d), BF16)],
        compiler_params=_cparams(("parallel", "arbitrary")),
        name="inproj",
    )(x, g, w)


def _qkv_kernel(p_ref, qg_ref, kg_ref, qn_ref, kn_ref, kt_ref, vb_ref, v_ref, *, n_heads):
    ones = _group_ones(LANES, DH_A)
    w_a = n_heads * DV_A
    scale = DH_A ** -0.5
    for h in range(n_heads):
        sl = slice(h * DV_A, (h + 1) * DV_A)
        q = p_ref[:, h * DV_A:(h + 1) * DV_A]
        qn = q * lax.rsqrt(_group_sum(q * q, ones) * (1.0 / DH_A) + RMS_EPS) * qg_ref[...]
        qn_ref[:, sl] = (qn * scale).astype(BF16)
        k = p_ref[:, w_a + h * DV_A:w_a + (h + 1) * DV_A]
        kn = k * lax.rsqrt(_group_sum(k * k, ones) * (1.0 / DH_A) + RMS_EPS) * kg_ref[...]
        kn_ref[:, sl] = kn
        kt_ref[h, 0] = kn.T.astype(BF16)
        v = p_ref[:, 2 * w_a + h * DV_A:2 * w_a + (h + 1) * DV_A]
        v_ref[:, sl] = v
        vb_ref[:, sl] = v.astype(BF16)


def _qkv_prep(p, qg, kg, n_heads):
    n = p.shape[0]
    bm = ATT_BLK
    w_a = n_heads * DV_A
    nb = n // bm
    row = lambda i: (i, 0)
    return pl.pallas_call(
        functools.partial(_qkv_kernel, n_heads=n_heads),
        grid=(nb,),
        in_specs=[
            pl.BlockSpec((bm, 3 * w_a), row),
            pl.BlockSpec((1, DV_A), lambda i: (0, 0)),
            pl.BlockSpec((1, DV_A), lambda i: (0, 0)),
        ],
        out_specs=[
            pl.BlockSpec((bm, w_a), row),
            pl.BlockSpec((bm, w_a), row),
            pl.BlockSpec((n_heads, 1, DV_A, bm), lambda i: (0, i, 0, 0)),
            pl.BlockSpec((bm, w_a), row),
            pl.BlockSpec((bm, w_a), row),
        ],
        out_shape=[
            jax.ShapeDtypeStruct((n, w_a), BF16),
            jax.ShapeDtypeStruct((n, w_a), F32),
            jax.ShapeDtypeStruct((n_heads, nb, DV_A, bm), BF16),
            jax.ShapeDtypeStruct((n, w_a), BF16),
            jax.ShapeDtypeStruct((n, w_a), F32),
        ],
        compiler_params=_cparams(("parallel",)),
        name="qkv_prep",
    )(p, qg, kg)


def _t5_bucket(n):
    max_exact = NUM_BUCKETS // 2
    nf = jnp.maximum(n, 1).astype(F32)
    large = max_exact + (jnp.log(nf / max_exact) / math.log(MAX_DISTANCE / max_exact)
                         * (NUM_BUCKETS - max_exact)).astype(jnp.int32)
    large = jnp.minimum(large, NUM_BUCKETS - 1)
    return jnp.where(n < max_exact, n, large)


def _bias_from_distance(n, rb_ref, h, n_heads):
    bucket = _t5_bucket(jnp.maximum(n, 0))
    far = rb_ref[(NUM_BUCKETS - 1) * n_heads + h]
    val = jnp.zeros(n.shape, F32)
    for b in range(NUM_BUCKETS - 1):
        val = jnp.where(bucket == b, rb_ref[b * n_heads + h] - far, val)
    return jnp.where(n < 0, NEG, val)


def _bias_kernel(rb_ref, near_ref, last_ref, new_ref, *, n_heads, blk, page, dec_seq):
    r = lax.broadcasted_iota(jnp.int32, (blk, 2 * blk), 0)
    c = lax.broadcasted_iota(jnp.int32, (blk, 2 * blk), 1)
    n_near = r - c + blk
    rows = 2 * dec_seq
    t = lax.broadcasted_iota(jnp.int32, (rows, LANES), 0) % dec_seq
    j = lax.broadcasted_iota(jnp.int32, (rows, LANES), 1)
    n_last = page + t - j
    n_new = jnp.where(j < dec_seq, t - j, -1)
    for h in range(n_heads):
        near_ref[h] = _bias_from_distance(n_near, rb_ref, h, n_heads)
        last_ref[h] = _bias_from_distance(n_last, rb_ref, h, n_heads)
        new_ref[h] = _bias_from_distance(n_new, rb_ref, h, n_heads)


def _bias_tiles(rel_bias, n_heads, blk, page, dec_seq):
    rows = 2 * dec_seq
    return pl.pallas_call(
        functools.partial(_bias_kernel, n_heads=n_heads, blk=blk, page=page, dec_seq=dec_seq),
        in_specs=[pl.BlockSpec(memory_space=pltpu.SMEM)],
        out_shape=[
            jax.ShapeDtypeStruct((n_heads, blk, 2 * blk), F32),
            jax.ShapeDtypeStruct((n_heads, rows, LANES), F32),
            jax.ShapeDtypeStruct((n_heads, rows, LANES), F32),
        ],
        compiler_params=pltpu.CompilerParams(vmem_limit_bytes=VMEM_LIMIT),
        name="bias_tiles",
    )(rel_bias.reshape(-1))


def _lambda(lam_ref, lam_init):
    l = lam_ref[...]
    s1 = jnp.sum(l[0:1] * l[1:2], axis=-1, keepdims=True)
    s2 = jnp.sum(l[2:3] * l[3:4], axis=-1, keepdims=True)
    return jnp.exp(s1) - jnp.exp(s2) + lam_init


def _softmax_step(s, m, l, acc, v):
    m_new = jnp.maximum(m, jnp.max(s, axis=-1, keepdims=True))
    p = jnp.exp(s - m_new)
    alpha = jnp.exp(m - m_new)
    l = alpha * l + jnp.sum(p, axis=-1, keepdims=True)
    acc = alpha * acc + _dot(p.astype(BF16), v)
    return m_new, l, acc


def _diff_finish(o1, o2, lam, g, lam_init):
    o = o1 - lam * o2
    ms = jnp.mean(o * o, axis=-1, keepdims=True)
    return o * lax.rsqrt(ms + RMS_EPS) * g * (1.0 - lam_init)


def _attn_prompt_kernel(q_ref, kt_ref, v_ref, bias_ref, lam_ref, g_ref, o_ref, *, lam_init):
    i = pl.program_id(1)
    blk = q_ref.shape[0]
    q = q_ref[...]
    lane = lax.broadcasted_iota(jnp.int32, q.shape, 1)
    zero = jnp.zeros_like(q)
    qc = (jnp.where(lane < DH_A, q, zero), jnp.where(lane >= DH_A, q, zero))

    def v_block(kb):
        return v_ref[pl.ds(pl.multiple_of(kb * blk, blk), blk), :]

    def far_step(kb, carry):
        kt = kt_ref[0, kb]
        v = v_block(kb)
        out = []
        for c in range(2):
            m, l, acc = carry[3 * c:3 * c + 3]
            out.extend(_softmax_step(_dot(qc[c], kt), m, l, acc, v))
        return tuple(out)

    init = (jnp.full((blk, 1), NEG, F32), jnp.zeros((blk, 1), F32), jnp.zeros((blk, DV_A), F32)) * 2
    carry = lax.fori_loop(0, jnp.maximum(i - 1, 0), far_step, init)

    kb0 = jnp.maximum(i - 1, 0)
    bias = bias_ref[0]
    col = lax.broadcasted_iota(jnp.int32, bias.shape, 1)
    bias = jnp.where(jnp.logical_and(i == 0, col < blk), NEG, bias)
    kt0, kt1 = kt_ref[0, kb0], kt_ref[0, i]
    v01 = jnp.concatenate([v_block(kb0), v_block(i)], axis=0)
    outs = []
    for c in range(2):
        m, l, acc = carry[3 * c:3 * c + 3]
        s = jnp.concatenate([_dot(qc[c], kt0), _dot(qc[c], kt1)], axis=1) + bias
        m, l, acc = _softmax_step(s, m, l, acc, v01)
        outs.append(acc / l)
    lam = _lambda(lam_ref, lam_init)
    o_ref[...] = _diff_finish(outs[0], outs[1], lam, g_ref[...], lam_init)


def _attn_prompt(qn, kt, vb, bias_near, lamv, g, ya, t_prompt, n_heads, lam_init):
    blk = ATT_BLK
    nq = t_prompt // blk
    kernel = functools.partial(_attn_prompt_kernel, lam_init=lam_init)
    return pl.pallas_call(
        kernel,
        grid=(n_heads, nq),
        in_specs=[
            pl.BlockSpec((blk, DV_A), lambda h, i: (i, h)),
            pl.BlockSpec((1, nq, DV_A, blk), lambda h, i: (h, 0, 0, 0)),
            pl.BlockSpec((t_prompt, DV_A), lambda h, i: (0, h)),
            pl.BlockSpec((1, blk, 2 * blk), lambda h, i: (h, 0, 0)),
            pl.BlockSpec((4, DH_A), lambda h, i: (0, 0)),
            pl.BlockSpec((1, DV_A), lambda h, i: (0, 0)),
        ],
        out_specs=pl.BlockSpec((blk, DV_A), lambda h, i: (i, h)),
        out_shape=ya,
        compiler_params=_cparams(("parallel", "arbitrary")),
        name="attn_prompt",
    )(qn, kt, vb, bias_near, lamv, g)


def _attn_sample_kernel(pt_ref, q_ref, qg_ref, kn_ref, vn_ref, kc_ref, vc_ref, blast_ref, bnew_ref,
                        lam_ref, g_ref, ya_in_ref, o_ref, wq_ref, m_ref, l_ref, acc_ref,
                        *, n_heads, lam_init):
    del pt_ref, ya_in_ref
    p = pl.program_id(1)
    last = pl.num_programs(1) - 1
    dec_seq = q_ref.shape[0]
    rows = 2 * dec_seq
    lane = lax.broadcasted_iota(jnp.int32, (dec_seq, DV_A), 1)

    @pl.when(p == 0)
    def _():
        ones = _group_ones(LANES, DH_A)
        scale = DH_A ** -0.5
        for h in range(n_heads):
            q = q_ref[:, h * DV_A:(h + 1) * DV_A]
            qn = q * lax.rsqrt(_group_sum(q * q, ones) * (1.0 / DH_A) + RMS_EPS) * qg_ref[...] * scale
            wq = jnp.concatenate([jnp.where(lane < DH_A, qn, 0.0), jnp.where(lane >= DH_A, qn, 0.0)], axis=0)
            wq_ref[h] = wq.astype(BF16)
        m_ref[...] = jnp.full(m_ref.shape, NEG, F32)
        l_ref[...] = jnp.zeros(l_ref.shape, F32)
        acc_ref[...] = jnp.zeros(acc_ref.shape, F32)

    def update(h, k, v, bias):
        s = lax.dot_general(wq_ref[h], k, _NT, preferred_element_type=F32)
        if bias is not None:
            s = s + bias
        m, l, acc = _softmax_step(s, m_ref[h][:, 0:1], l_ref[h][:, 0:1], acc_ref[h], v)
        m_ref[h] = jnp.broadcast_to(m, (rows, LANES))
        l_ref[h] = jnp.broadcast_to(l, (rows, LANES))
        acc_ref[h] = acc

    is_last = p == last
    for h in range(n_heads):
        k = kc_ref[0, 0, :, h, :].astype(BF16)
        v = vc_ref[0, 0, :, h, :].astype(BF16)
        update(h, k, v, jnp.where(is_last, blast_ref[h], 0.0))

    @pl.when(is_last)
    def _():
        lam = _lambda(lam_ref, lam_init)
        pad = jnp.zeros((LANES - dec_seq, DV_A), F32)
        for h in range(n_heads):
            sl = slice(h * DV_A, (h + 1) * DV_A)
            k = jnp.concatenate([kn_ref[:, sl], pad], axis=0).astype(BF16)
            v = jnp.concatenate([vn_ref[:, sl], pad], axis=0).astype(BF16)
            update(h, k, v, bnew_ref[h])
            o = acc_ref[h] / l_ref[h][:, 0:1]
            o_ref[:, sl] = _diff_finish(o[0:dec_seq], o[dec_seq:rows], lam, g_ref[...], lam_init)


def _attn_sample(page_table, p, qg, kn, cache_k, cache_v, layer, blast, bnew, lamv, g, ya,
                 t_prompt, dec_seq, n_heads, lam_init):
    n_seq, n_pages = page_table.shape
    page = cache_k.shape[2]
    w_a = n_heads * DV_A
    rows = 2 * dec_seq
    row0 = t_prompt // dec_seq
    new_rows = lambda col: (lambda b, pg, pt: (row0 + b, col))
    const = lambda *shape: pl.BlockSpec(shape, lambda b, pg, pt: (0,) * len(shape))
    cache_spec = pl.BlockSpec((1, 1, page, n_heads, DV_A), lambda b, pg, pt: (layer, pt[b, pg], 0, 0, 0))
    kernel = functools.partial(_attn_sample_kernel, n_heads=n_heads, lam_init=lam_init)
    return pl.pallas_call(
        kernel,
        grid_spec=pltpu.PrefetchScalarGridSpec(
            num_scalar_prefetch=1,
            grid=(n_seq, n_pages),
            in_specs=[
                pl.BlockSpec((dec_seq, w_a), new_rows(0)),
                const(1, DV_A),
                pl.BlockSpec((dec_seq, w_a), new_rows(0)),
                pl.BlockSpec((dec_seq, w_a), new_rows(2)),
                cache_spec,
                cache_spec,
                const(n_heads, rows, LANES),
                const(n_heads, rows, LANES),
                const(4, DH_A),
                const(1, DV_A),
                pl.BlockSpec(memory_space=pl.ANY),
            ],
            out_specs=pl.BlockSpec((dec_seq, w_a), new_rows(0)),
            scratch_shapes=[
                pltpu.VMEM((n_heads, rows, DV_A), BF16),
                pltpu.VMEM((n_heads, rows, LANES), F32),
                pltpu.VMEM((n_heads, rows, LANES), F32),
                pltpu.VMEM((n_heads, rows, DV_A), F32),
            ],
        ),
        out_shape=jax.ShapeDtypeStruct(ya.shape, ya.dtype),
        input_output_aliases={11: 0},
        compiler_params=_cparams(("parallel", "arbitrary")),
        name="attn_sample",
    )(page_table, p, qg, kn, p, cache_k, cache_v, blast, bnew, lamv, g, ya)


def _rwkv_prep_kernel(x_ref, prev_ref, ov_ref, mu_ref, w0_ref, a0_ref, kk_ref, ka_ref, rk_ref,
                      w2_ref, a2_ref, g2_ref,
                      r_ref, lw_ref, k_ref, v_ref, kkn_ref, a_ref, g_ref, bonus_ref,
                      *, n_prompt_tiles, dec_seq, w_b):
    i = pl.program_id(0)
    x = x_ref[...]
    rows = lax.broadcasted_iota(jnp.int32, x.shape, 0)
    prev = pltpu.roll(x, 1, axis=0)
    first = jnp.where(i == 0, 0.0, prev_ref[SUBLANES - 1:SUBLANES, :])
    prev = jnp.where(rows == 0, first, prev)
    seq_start = jnp.logical_and(i >= n_prompt_tiles, rows % dec_seq == 0)
    prev = jnp.where(seq_start, ov_ref[...], prev)
    xs = x + (prev - x) * mu_ref[...]

    r = xs[:, 0:w_b]
    k = xs[:, w_b:2 * w_b]
    v = xs[:, 2 * w_b:3 * w_b]
    lora = xs[:, 3 * w_b:3 * w_b + LORA_W]
    z = -(w0_ref[...] + _dot(jnp.tanh(lora).astype(BF16), w2_ref[...]))
    softplus = jnp.maximum(z, 0.0) + jnp.log(1.0 + jnp.exp(-jnp.abs(z)))
    lw_ref[...] = -jnp.exp(-softplus - 0.5)
    a = _sigmoid(a0_ref[...] + _dot(lora.astype(BF16), a2_ref[...]))
    g_ref[...] = _dot(_sigmoid(lora).astype(BF16), g2_ref[...])
    ones = _group_ones(LANES, DH_B)
    kk = k * kk_ref[...]
    k2 = k * (1.0 + (a - 1.0) * ka_ref[...])
    rkr = r * k2 * rk_ref[...]
    for c in range(w_b // LANES):
        sl = slice(c * LANES, (c + 1) * LANES)
        kc = kk[:, sl]
        nrm = jnp.sqrt(_group_sum(kc * kc, ones))
        kkn_ref[:, sl] = kc / jnp.maximum(nrm, 1e-12)
        bonus_ref[:, sl] = _group_sum(rkr[:, sl], ones) * v[:, sl]
    r_ref[...] = r
    k_ref[...] = k2
    v_ref[...] = v
    a_ref[...] = a


def _rwkv_prep(p, col_blk, ov, mu, w0, a0, kk, ka, rk, w2p, a2p, g2p, t_prompt, dec_seq):
    n = p.shape[0]
    bm = ATT_BLK
    w_b = w0.shape[1]
    width = mu.shape[1]
    n_pt = t_prompt // bm
    vec = lambda w: pl.BlockSpec((1, w), lambda i: (0, 0))
    mat = pl.BlockSpec((LORA_W, w_b), lambda i: (0, 0))
    out = pl.BlockSpec((bm, w_b), lambda i: (i, 0))
    kernel = functools.partial(_rwkv_prep_kernel, n_prompt_tiles=n_pt, dec_seq=dec_seq, w_b=w_b)
    return pl.pallas_call(
        kernel,
        grid=(n // bm,),
        in_specs=[
            pl.BlockSpec((bm, width), lambda i: (i, col_blk)),
            pl.BlockSpec((SUBLANES, width), lambda i: (jnp.maximum(i * (bm // SUBLANES) - 1, 0), col_blk)),
            pl.BlockSpec((bm, width), lambda i: (jnp.maximum(i - n_pt, 0), 0)),
            vec(width), vec(w_b), vec(w_b), vec(w_b), vec(w_b), vec(w_b), mat, mat, mat,
        ],
        out_specs=[out] * 8,
        out_shape=[jax.ShapeDtypeStruct((n, w_b), F32)] * 8,
        compiler_params=_cparams(("parallel",)),
        name="rwkv_prep",
    )(p, p, ov, mu, w0, a0, kk, ka, rk, w2p, a2p, g2p)


def _rwkv_scan_kernel(r_ref, lw_ref, k_ref, v_ref, kk_ref, a_ref, g_ref, bonus_ref, s0_ref,
                      lng_ref, lnb_ref, yb_in_ref, y_ref, sout_ref, s_ref, *, n_heads):
    del yb_in_ref
    c_idx = pl.program_id(1)
    chunk = r_ref.shape[0]

    @pl.when(c_idx == 0)
    def _():
        s_ref[...] = s0_ref[0]

    ti = lax.broadcasted_iota(jnp.int32, (chunk, chunk), 0)
    si = lax.broadcasted_iota(jnp.int32, (chunk, chunk), 1)
    lower = ti >= si
    strict = ti > si

    lw = lw_ref[...]
    cum = _dot_hi(lower.astype(F32), lw)
    g_prev = jnp.exp(cum - lw)
    g_cur = jnp.exp(cum)
    g_inv = jnp.exp(-cum)
    cum_end = cum[chunk - 1:chunk, :]
    g_end = jnp.exp(cum_end - cum)
    g_all = jnp.exp(cum_end)
    kk = kk_ref[...]
    b = kk * a_ref[...]
    k = k_ref[...]
    a_t = -kk * g_prev
    r_t = r_ref[...] * g_cur
    b_t = b * g_inv
    k_t = k * g_inv
    b_e = b * g_end
    k_e = k * g_end
    v_all = v_ref[...]
    n_double = max(int(math.log2(chunk)), 1)

    for h in range(n_heads):
        sl = slice(h * DH_B, (h + 1) * DH_B)
        ah, rh, bh, kh, vh = a_t[:, sl], r_t[:, sl], b_t[:, sl], k_t[:, sl], v_all[:, sl]
        l_ab = jnp.where(strict, _dot_hi(ah, bh, _NT), 0.0)
        l_ak = jnp.where(strict, _dot_hi(ah, kh, _NT), 0.0)
        m_rb = jnp.where(lower, _dot_hi(rh, bh, _NT), 0.0)
        m_rk = jnp.where(lower, _dot_hi(rh, kh, _NT), 0.0)
        s = s_ref[h]
        x = _dot_hi(ah, s, _NT) + _dot_hi(l_ak, vh)
        lp = l_ab
        x = x + _dot_hi(lp, x)
        for _ in range(n_double - 1):
            lp = _dot_hi(lp, lp)
            x = x + _dot_hi(lp, x)
        u = x
        y = _dot_hi(rh, s, _NT) + _dot_hi(m_rb, u) + _dot_hi(m_rk, vh)
        s_new = s * g_all[:, sl] + _dot_hi(u, b_e[:, sl], _TN) + _dot_hi(vh, k_e[:, sl], _TN)
        s_ref[h] = s_new
        sout_ref[0, h] = s_new
        mean = jnp.mean(y, axis=-1, keepdims=True)
        yc = y - mean
        y_ref[:, sl] = yc * lax.rsqrt(jnp.mean(yc * yc, axis=-1, keepdims=True) + LNX_EPS)

    y_ref[...] = (y_ref[...] * lng_ref[...] + lnb_ref[...] + bonus_ref[...]) * g_ref[...]


def _rwkv_scan(streams, s0, lng, lnb, yb, row0, n_seq, seq_len, chunk):
    n, w_b = streams[0].shape
    n_heads = w_b // DH_B
    n_chunks = seq_len // chunk
    blk0 = row0 // chunk
    rows = pl.BlockSpec((chunk, w_b), lambda b, c: (blk0 + b * n_chunks + c, 0))
    vec = pl.BlockSpec((1, w_b), lambda b, c: (0, 0))
    state = pl.BlockSpec((1, n_heads, DH_B, DH_B), lambda b, c: (b, 0, 0, 0))
    in_specs = [rows] * 8 + [state, vec, vec]
    args = list(streams) + [s0, lng, lnb]
    aliases = {}
    if yb is not None:
        in_specs.append(pl.BlockSpec(memory_space=pl.ANY))
        args.append(yb)
        aliases = {11: 0}
    kernel = functools.partial(_rwkv_scan_kernel, n_heads=n_heads)
    if yb is None:
        kernel = functools.partial(_scan_no_alias, kernel)
    return pl.pallas_call(
        kernel,
        grid=(n_seq, n_chunks),
        in_specs=in_specs,
        out_specs=[rows, state],
        out_shape=[jax.ShapeDtypeStruct((n, w_b), F32),
                   jax.ShapeDtypeStruct((n_seq, n_heads, DH_B, DH_B), F32)],
        scratch_shapes=[pltpu.VMEM((n_heads, DH_B, DH_B), F32)],
        input_output_aliases=aliases,
        compiler_params=_cparams(("parallel", "arbitrary")),
        name="rwkv_scan",
    )(*args)


def _scan_no_alias(kernel, *refs):
    return kernel(*refs[:11], None, *refs[11:])


def _glu(pc, w_c):
    return pc[:, 0:w_c] * _sigmoid(pc[:, w_c:2 * w_c])


def _conv_norm_act(c, b_ref, lg_ref, lb_ref):
    c = c + b_ref[...]
    mu = jnp.mean(c, axis=-1, keepdims=True)
    cc = c - mu
    var = jnp.mean(cc * cc, axis=-1, keepdims=True)
    c = cc * lax.rsqrt(var + LN_EPS) * lg_ref[...] + lb_ref[...]
    return c * _sigmoid(c)


def _conv_prompt_kernel(pc_ref, halo_ref, w_ref, b_ref, lg_ref, lb_ref, y_ref, st_ref, buf_ref, *, sub):
    i = pl.program_id(0)
    bm, w_c = y_ref.shape
    halo = halo_ref.shape[0]
    buf_ref[0:halo, :] = jnp.where(i == 0, 0.0, _glu(halo_ref[...], w_c))
    buf_ref[halo:halo + bm, :] = _glu(pc_ref[...], w_c)
    off = halo - (CONV_K - 1)
    for s in range(bm // sub):
        acc = jnp.zeros((sub, w_c), F32)
        for j in range(CONV_K):
            acc = acc + w_ref[j:j + 1, :] * buf_ref[s * sub + off + j:s * sub + off + j + sub, :]
        y_ref[s * sub:(s + 1) * sub, :] = _conv_norm_act(acc, b_ref, lg_ref, lb_ref)
    st_ref[0] = buf_ref[halo + bm - (CONV_K - 1):halo + bm, :]


def _conv_prompt(p, col_blk, w, b, lg, lb, n_rows, t_prompt):
    bm = ATT_BLK
    halo = 32
    w_c = w.shape[1]
    vec = pl.BlockSpec((1, w_c), lambda i: (0, 0))
    return pl.pallas_call(
        functools.partial(_conv_prompt_kernel, sub=64),
        grid=(t_prompt // bm,),
        in_specs=[
            pl.BlockSpec((bm, 2 * w_c), lambda i: (i, col_blk)),
            pl.BlockSpec((halo, 2 * w_c), lambda i: (jnp.maximum(i * (bm // halo) - 1, 0), col_blk)),
            pl.BlockSpec((CONV_K, w_c), lambda i: (0, 0)),
            vec, vec, vec,
        ],
        out_specs=[
            pl.BlockSpec((bm, w_c), lambda i: (i, 0)),
            pl.BlockSpec((1, CONV_K - 1, w_c), lambda i: (0, 0, 0)),
        ],
        out_shape=[
            jax.ShapeDtypeStruct((n_rows, w_c), F32),
            jax.ShapeDtypeStruct((1, CONV_K - 1, w_c), F32),
        ],
        scratch_shapes=[pltpu.VMEM((halo + bm, w_c), F32)],
        compiler_params=_cparams(("arbitrary",)),
        name="conv_prompt",
    )(p, p, w, b, lg, lb)


def _conv_sample_kernel(pc_ref, st0_ref, w_ref, b_ref, lg_ref, lb_ref, yc_in_ref, y_ref, st_ref, buf_ref):
    del yc_in_ref
    dec_seq, w_c = y_ref.shape
    hist = CONV_K - 1
    buf_ref[0:hist, :] = st0_ref[0]
    buf_ref[hist:hist + dec_seq, :] = _glu(pc_ref[...], w_c)
    acc = jnp.zeros((dec_seq, w_c), F32)
    for j in range(CONV_K):
        acc = acc + w_ref[j:j + 1, :] * buf_ref[j:j + dec_seq, :]
    y_ref[...] = _conv_norm_act(acc, b_ref, lg_ref, lb_ref)
    st_ref[0] = buf_ref[dec_seq:dec_seq + hist, :]


def _conv_sample(p, col_blk, st0, w, b, lg, lb, yc, t_prompt, dec_seq):
    n_seq = st0.shape[0]
    w_c = w.shape[1]
    hist = CONV_K - 1
    row0 = t_prompt // dec_seq
    vec = pl.BlockSpec((1, w_c), lambda s: (0, 0))
    return pl.pallas_call(
        _conv_sample_kernel,
        grid=(n_seq,),
        in_specs=[
            pl.BlockSpec((dec_seq, 2 * w_c), lambda s: (row0 + s, col_blk)),
            pl.BlockSpec((1, hist, w_c), lambda s: (s, 0, 0)),
            pl.BlockSpec((CONV_K, w_c), lambda s: (0, 0)),
            vec, vec, vec,
            pl.BlockSpec(memory_space=pl.ANY),
        ],
        out_specs=[
            pl.BlockSpec((dec_seq, w_c), lambda s: (row0 + s, 0)),
            pl.BlockSpec((1, hist, w_c), lambda s: (s, 0, 0)),
        ],
        out_shape=[
            jax.ShapeDtypeStruct(yc.shape, yc.dtype),
            jax.ShapeDtypeStruct((n_seq, hist, w_c), F32),
        ],
        scratch_shapes=[pltpu.VMEM((hist + dec_seq + 2, w_c), F32)],
        input_output_aliases={6: 0},
        compiler_params=_cparams(("parallel",)),
        name="conv_sample",
    )(p, st0, w, b, lg, lb, yc)


def _outproj_kernel(x_ref, ya_ref, yb_ref, yc_ref, wa_ref, wb_ref, wc_ref, o_ref):
    acc = x_ref[...] + _dot(ya_ref[...].astype(BF16), wa_ref[...])
    acc = acc + _dot(yb_ref[...].astype(BF16), wb_ref[...])
    o_ref[...] = acc + _dot(yc_ref[...].astype(BF16), wc_ref[...])


def _outproj(x, ya, yb, yc, wo, bm, bn):
    n, d = x.shape
    wa, wb, wc = ya.shape[1], yb.shape[1], yc.shape[1]
    row = lambda w: pl.BlockSpec((bm, w), lambda i, j: (i, 0))
    return pl.pallas_call(
        _outproj_kernel,
        grid=(n // bm, d // bn),
        in_specs=[
            pl.BlockSpec((bm, bn), lambda i, j: (i, j)),
            row(wa), row(wb), row(wc),
            pl.BlockSpec((wa, bn), lambda i, j: (0, j)),
            pl.BlockSpec((wb, bn), lambda i, j: (wa // wb, j)),
            pl.BlockSpec((wc, bn), lambda i, j: ((wa + wb) // wc, j)),
        ],
        out_specs=pl.BlockSpec((bm, bn), lambda i, j: (i, j)),
        out_shape=jax.ShapeDtypeStruct((n, d), F32),
        compiler_params=_cparams(("parallel", "arbitrary")),
        name="outproj",
    )(x, ya, yb, yc, wo, wo, wo)


def _ffn_kernel(x_ref, g_ref, wu_ref, wd_ref, o_ref, h_ref):
    f = pl.program_id(1)

    @pl.when(f == 0)
    def _():
        x = x_ref[...]
        ms = jnp.mean(x * x, axis=-1, keepdims=True)
        h_ref[...] = (x * lax.rsqrt(ms + RMS_EPS) * g_ref[...]).astype(BF16)
        o_ref[...] = x

    u = jnp.maximum(_dot(h_ref[...], wu_ref[...]), 0.0)
    o_ref[...] += _dot((u * u).astype(BF16), wd_ref[...])


def _ffn(x, g, wu, wd, bm, bf):
    n, d = x.shape
    d_ff = wu.shape[1]
    return pl.pallas_call(
        _ffn_kernel,
        grid=(n // bm, d_ff // bf),
        in_specs=[
            pl.BlockSpec((bm, d), lambda i, f: (i, 0)),
            pl.BlockSpec((1, d), lambda i, f: (0, 0)),
            pl.BlockSpec((d, bf), lambda i, f: (0, f)),
            pl.BlockSpec((bf, d), lambda i, f: (f, 0)),
        ],
        out_specs=pl.BlockSpec((bm, d), lambda i, f: (i, 0)),
        out_shape=jax.ShapeDtypeStruct((n, d), F32),
        scratch_shapes=[pltpu.VMEM((bm, d), BF16)],
        compiler_params=_cparams(("parallel", "arbitrary")),
        name="ffn",
    )(x, g, wu, wd)


def _row_tile(n, target):
    best = SUBLANES
    for t in range(SUBLANES, target + 1, SUBLANES):
        if n % t == 0:
            best = t
    return best


def _lambda_init(l):
    return 0.8 - 0.6 * math.exp(-0.3 * l)


def _pad_rows(w, start, total):
    return jnp.zeros((total, w.shape[1]), w.dtype).at[start:start + w.shape[0]].set(w)


def kernel(x_prompt, x_sample, cache_k, cache_v, state_rwkv, state_shift, state_conv, page_table, rel_bias, norm_mix_g, w_in, q_norm_g, k_norm_g, lambda_q1, lambda_k1, lambda_q2, lambda_k2, subln_g, rwkv_mu, rwkv_w0, rwkv_w2, rwkv_a0, rwkv_a2, rwkv_g2, rwkv_k_k, rwkv_k_a, rwkv_r_k, rwkv_lnx_g, rwkv_lnx_b, conv_dw_w, conv_dw_b, conv_ln_g, conv_ln_b, w_out, norm_ffn_g, w_up, w_down):
    depth = w_in.shape[0]
    b_p, t_p, d = x_prompt.shape
    n_seq, dec_seq, _ = x_sample.shape
    assert b_p == 1 and dec_seq == SUBLANES and t_p % ATT_BLK == 0 and (n_seq * dec_seq) % ATT_BLK == 0
    w_a, w_b, w_c = d // 2, d // 4, d // 4
    n_heads = w_a // DV_A
    h_b = w_b // DH_B
    shift_w = state_shift.shape[-1]
    assert shift_w == 3 * w_b + DECAY_LORA + AAA_LORA + GATE_LORA and shift_w <= 3 * w_b + LORA_W
    shift_pad = 3 * w_b + LORA_W
    page = cache_k.shape[2]
    assert page == LANES and page >= MAX_DISTANCE and ATT_BLK >= MAX_DISTANCE
    n = t_p + n_seq * dec_seq
    bm = _row_tile(n, 1024)

    x = jnp.concatenate([x_prompt.reshape(t_p, d), x_sample.reshape(n_seq * dec_seq, d)], axis=0)
    bias_near, bias_last, bias_new = _bias_tiles(rel_bias, n_heads, ATT_BLK, page, dec_seq)

    a_end, b_end = 3 * w_a, 3 * w_a + shift_w
    col_c = 3 * w_a // (2 * w_c)
    col_b = (3 * w_a + 2 * w_c) // shift_pad
    assert col_c * 2 * w_c == 3 * w_a and col_b * shift_pad == 3 * w_a + 2 * w_c

    ks_l, vs_l, kp_l, vp_l, sp_l, ss_l, shp_l, shs_l, cvp_l, cvs_l = ([] for _ in range(10))
    for l in range(depth):
        w_l = w_in[l]
        w_in_p = jnp.concatenate(
            [w_l[:, :a_end], w_l[:, b_end:], w_l[:, a_end:b_end], jnp.zeros((d, shift_pad - shift_w), w_l.dtype)],
            axis=1).astype(BF16)
        p = _inproj(x, norm_mix_g[l][None], w_in_p, bm, 1024)

        lam_init = _lambda_init(l)
        qg = jnp.tile(q_norm_g[l], 2)[None]
        kg = jnp.tile(k_norm_g[l], 2)[None]
        lamv = jnp.stack([lambda_q1[l], lambda_k1[l], lambda_q2[l], lambda_k2[l]])
        sg = subln_g[l][None]
        qn, kn, kt, vb, vf = _qkv_prep(p, qg, kg, n_heads)
        ya = _attn_prompt(qn, kt, vb, bias_near, lamv, sg, jax.ShapeDtypeStruct((n, w_a), F32),
                          t_p, n_heads, lam_init)
        ya = _attn_sample(page_table, p, qg, kn, cache_k, cache_v, l, bias_last, bias_new, lamv, sg, ya,
                          t_p, dec_seq, n_heads, lam_init)
        kp_l.append(kn[:t_p].reshape(b_p, t_p, n_heads, DV_A))
        vp_l.append(vf[:t_p].reshape(b_p, t_p, n_heads, DV_A))
        ks_l.append(kn[t_p:].reshape(n_seq, dec_seq, n_heads, DV_A))
        vs_l.append(vf[t_p:].reshape(n_seq, dec_seq, n_heads, DV_A))

        pad = ((0, 0), (0, shift_pad - shift_w))
        ov = jnp.pad(jnp.repeat(state_shift[l], dec_seq, axis=0), pad)
        lo = 0
        w2p = _pad_rows(rwkv_w2[l], lo, LORA_W).astype(BF16)
        a2p = _pad_rows(rwkv_a2[l], lo + DECAY_LORA, LORA_W).astype(BF16)
        g2p = _pad_rows(rwkv_g2[l], lo + DECAY_LORA + AAA_LORA, LORA_W).astype(BF16)
        streams = _rwkv_prep(p, col_b, ov, jnp.pad(rwkv_mu[l][None], pad), rwkv_w0[l][None], rwkv_a0[l][None],
                             rwkv_k_k[l][None], rwkv_k_a[l][None], rwkv_r_k[l].reshape(1, w_b),
                             w2p, a2p, g2p, t_p, dec_seq)
        lng, lnb = rwkv_lnx_g[l][None], rwkv_lnx_b[l][None]
        yb, s_p = _rwkv_scan(streams, jnp.zeros((b_p, h_b, DH_B, DH_B), F32), lng, lnb, None,
                             0, b_p, t_p, RW_CHUNK)
        yb, s_s = _rwkv_scan(streams, state_rwkv[l], lng, lnb, yb, t_p, n_seq, dec_seq, dec_seq)
        sp_l.append(s_p)
        ss_l.append(s_s)
        p_shift = p[:, col_b * shift_pad:col_b * shift_pad + shift_w]
        shp_l.append(p_shift[t_p - 1:t_p])
        shs_l.append(p_shift[t_p + dec_seq - 1::dec_seq])

        cw, cb = conv_dw_w[l], conv_dw_b[l][None]
        clg, clb = conv_ln_g[l][None], conv_ln_b[l][None]
        yc, cv_p = _conv_prompt(p, col_c, cw, cb, clg, clb, n, t_p)
        yc, cv_s = _conv_sample(p, col_c, state_conv[l], cw, cb, clg, clb, yc, t_p, dec_seq)
        cvp_l.append(cv_p)
        cvs_l.append(cv_s)

        x = _outproj(x, ya, yb, yc, w_out[l].astype(BF16), bm, 1024)
        x = _ffn(x, norm_ffn_g[l][None], w_up[l].astype(BF16), w_down[l].astype(BF16), bm, 512)

    y_prompt = x[:t_p].reshape(b_p, t_p, d)
    y_sample = x[t_p:].reshape(n_seq, dec_seq, d)
    return (y_prompt, y_sample, jnp.stack(kp_l), jnp.stack(vp_l), jnp.stack(ks_l), jnp.stack(vs_l),
            jnp.stack(sp_l), jnp.stack(ss_l), jnp.stack(shp_l), jnp.stack(shs_l),
            jnp.stack(cvp_l), jnp.stack(cvs_l))
```

```python
import functools
import math

import jax
import jax.numpy as jnp
from jax import lax
from jax.experimental import pallas as pl
from jax.experimental.pallas import tpu as pltpu

F32 = jnp.float32
BF16 = jnp.bfloat16

DH_A = 64
DV_A = 2 * DH_A
DH_B = 64
DECAY_LORA = 96
AAA_LORA = 96
GATE_LORA = 256
LORA_W = 512
CONV_K = 31
NUM_BUCKETS = 32
MAX_DISTANCE = 128
RMS_EPS = 1e-6
LN_EPS = 1e-5
LNX_EPS = 64e-5
NEG = -1e30
LOG2E = math.log2(math.e)
Q_SCALE = DH_A ** -0.5 * LOG2E

LANES = 128
SUBLANES = 8
VMEM_LIMIT = 56 * 1024 * 1024

ATT_BLK = 256
RW_CHUNK = 64
SCAN_PASSES = 1
ATT_HEADS_PER_STEP = 4


def _cparams(sem):
    return pltpu.CompilerParams(dimension_semantics=sem, vmem_limit_bytes=VMEM_LIMIT)


def _dot(a, b):
    return jnp.dot(a, b, preferred_element_type=F32)


def _dot_hi(a, b, dims=(((1,), (0,)), ((), ()))):
    return lax.dot_general(a, b, dims, preferred_element_type=F32, precision=lax.Precision.HIGHEST)


_NT = (((1,), (1,)), ((), ()))
_TN = (((0,), (0,)), ((), ()))
_NN = (((1,), (0,)), ((), ()))


def _mm(a, b, dims=_NN, passes=1):
    dot = lambda x, y: lax.dot_general(x, y, dims, preferred_element_type=F32)
    ah, bh = a.astype(BF16), b.astype(BF16)
    out = dot(ah, bh)
    if passes == 3:
        al = (a - ah.astype(F32)).astype(BF16)
        bl = (b - bh.astype(F32)).astype(BF16)
        out = out + dot(ah, bl) + dot(al, bh)
    return out


def _group_ones(width, group):
    r = lax.broadcasted_iota(jnp.int32, (width, width), 0) // group
    c = lax.broadcasted_iota(jnp.int32, (width, width), 1) // group
    return (r == c).astype(BF16)


def _group_sum(x, ones):
    hi = x.astype(BF16)
    lo = (x - hi.astype(F32)).astype(BF16)
    return _dot(hi, ones) + _dot(lo, ones)


def _sigmoid(x):
    return 1.0 / (1.0 + jnp.exp(-x))


def _inproj_kernel(x_ref, g_ref, w_ref, o_ref, h_ref):
    @pl.when(pl.program_id(1) == 0)
    def _():
        x = x_ref[...]
        ms = jnp.mean(x * x, axis=-1, keepdims=True)
        h_ref[...] = (x * lax.rsqrt(ms + RMS_EPS) * g_ref[...]).astype(BF16)

    o_ref[...] = _dot(h_ref[...], w_ref[...])


def _inproj(x, g, w, bm, bn):
    n, d = x.shape
    wn = w.shape[1]
    return pl.pallas_call(
        _inproj_kernel,
        grid=(n // bm, wn // bn),
        in_specs=[
            pl.BlockSpec((bm, d), lambda i, j: (i, 0)),
            pl.BlockSpec((1, d), lambda i, j: (0, 0)),
            pl.BlockSpec((d, bn), lambda i, j: (0, j)),
        ],
        out_specs=pl.BlockSpec((bm, bn), lambda i, j: (i, j)),
        out_shape=jax.ShapeDtypeStruct((n, wn), F32),
        scratch_shapes=[pltpu.VMEM((bm, d), BF16)],
        compiler_params=_cparams(("parallel", "arbitrary")),
        name="inproj",
    )(x, g, w)


def _qkv_kernel(p_ref, qg_ref, kg_ref, qt_ref, kn_ref, kb_ref, vt_ref, v_ref, *, n_heads):
    ones = _group_ones(LANES, DH_A)
    w_a = n_heads * DV_A
    for h in range(n_heads):
        sl = slice(h * DV_A, (h + 1) * DV_A)
        q = p_ref[:, h * DV_A:(h + 1) * DV_A]
        qn = q * lax.rsqrt(_group_sum(q * q, ones) * (1.0 / DH_A) + RMS_EPS) * qg_ref[...]
        qt_ref[h] = (qn * Q_SCALE).T.astype(BF16)
        k = p_ref[:, w_a + h * DV_A:w_a + (h + 1) * DV_A]
        kn = k * lax.rsqrt(_group_sum(k * k, ones) * (1.0 / DH_A) + RMS_EPS) * kg_ref[...]
        kn_ref[:, sl] = kn
        kb_ref[:, sl] = kn.astype(BF16)
        v = p_ref[:, 2 * w_a + h * DV_A:2 * w_a + (h + 1) * DV_A]
        v_ref[:, sl] = v
        vt_ref[h, 0] = v.T.astype(BF16)


def _qkv_prep(p, qg, kg, n_heads):
    n = p.shape[0]
    bm = ATT_BLK
    w_a = n_heads * DV_A
    nb = n // bm
    row = lambda i: (i, 0)
    return pl.pallas_call(
        functools.partial(_qkv_kernel, n_heads=n_heads),
        grid=(nb,),
        in_specs=[
            pl.BlockSpec((bm, 3 * w_a), row),
            pl.BlockSpec((1, DV_A), lambda i: (0, 0)),
            pl.BlockSpec((1, DV_A), lambda i: (0, 0)),
        ],
        out_specs=[
            pl.BlockSpec((n_heads, DV_A, bm), lambda i: (0, 0, i)),
            pl.BlockSpec((bm, w_a), row),
            pl.BlockSpec((bm, w_a), row),
            pl.BlockSpec((n_heads, 1, DV_A, bm), lambda i: (0, i, 0, 0)),
            pl.BlockSpec((bm, w_a), row),
        ],
        out_shape=[
            jax.ShapeDtypeStruct((n_heads, DV_A, n), BF16),
            jax.ShapeDtypeStruct((n, w_a), F32),
            jax.ShapeDtypeStruct((n, w_a), BF16),
            jax.ShapeDtypeStruct((n_heads, nb, DV_A, bm), BF16),
            jax.ShapeDtypeStruct((n, w_a), F32),
        ],
        compiler_params=_cparams(("parallel",)),
        name="qkv_prep",
    )(p, qg, kg)


def _t5_bucket(n):
    max_exact = NUM_BUCKETS // 2
    nf = jnp.maximum(n, 1).astype(F32)
    large = max_exact + (jnp.log(nf / max_exact) / math.log(MAX_DISTANCE / max_exact)
                         * (NUM_BUCKETS - max_exact)).astype(jnp.int32)
    large = jnp.minimum(large, NUM_BUCKETS - 1)
    return jnp.where(n < max_exact, n, large)


def _bias_from_distance(n, rb_ref, h, n_heads):
    bucket = _t5_bucket(jnp.maximum(n, 0))
    far = rb_ref[(NUM_BUCKETS - 1) * n_heads + h]
    val = jnp.zeros(n.shape, F32)
    for b in range(NUM_BUCKETS - 1):
        val = jnp.where(bucket == b, (rb_ref[b * n_heads + h] - far) * LOG2E, val)
    return jnp.where(n < 0, NEG, val)


def _bias_kernel(rb_ref, near_ref, last_ref, new_ref, *, n_heads, blk, page, dec_seq):
    r = lax.broadcasted_iota(jnp.int32, (2 * blk, blk), 0)
    c = lax.broadcasted_iota(jnp.int32, (2 * blk, blk), 1)
    n_near = c - r + blk
    rows = 2 * dec_seq
    t = lax.broadcasted_iota(jnp.int32, (rows, LANES), 0) % dec_seq
    j = lax.broadcasted_iota(jnp.int32, (rows, LANES), 1)
    n_last = page + t - j
    n_new = jnp.where(j < dec_seq, t - j, -1)
    for h in range(n_heads):
        near_ref[h] = _bias_from_distance(n_near, rb_ref, h, n_heads)
        last_ref[h] = _bias_from_distance(n_last, rb_ref, h, n_heads)
        new_ref[h] = _bias_from_distance(n_new, rb_ref, h, n_heads)


def _bias_tiles(rel_bias, n_heads, blk, page, dec_seq):
    rows = 2 * dec_seq
    return pl.pallas_call(
        functools.partial(_bias_kernel, n_heads=n_heads, blk=blk, page=page, dec_seq=dec_seq),
        in_specs=[pl.BlockSpec(memory_space=pltpu.SMEM)],
        out_shape=[
            jax.ShapeDtypeStruct((n_heads, 2 * blk, blk), F32),
            jax.ShapeDtypeStruct((n_heads, rows, LANES), F32),
            jax.ShapeDtypeStruct((n_heads, rows, LANES), F32),
        ],
        compiler_params=pltpu.CompilerParams(vmem_limit_bytes=VMEM_LIMIT),
        name="bias_tiles",
    )(rel_bias.reshape(-1))


def _lambda(lam_ref, lam_init):
    l = lam_ref[...]
    s1 = jnp.sum(l[0:1] * l[1:2], axis=-1, keepdims=True)
    s2 = jnp.sum(l[2:3] * l[3:4], axis=-1, keepdims=True)
    return jnp.exp(s1) - jnp.exp(s2) + lam_init


def _softmax_update(s, m, l, axis):
    m_new = jnp.maximum(m, jnp.max(s, axis=axis, keepdims=True))
    p = jnp.exp2(s - m_new)
    alpha = jnp.exp2(m - m_new)
    return m_new, alpha * l + jnp.sum(p, axis=axis, keepdims=True), alpha, p.astype(BF16)


def _diff_finish(o1, o2, lam, g, lam_init, axis):
    o = o1 - lam * o2
    ms = jnp.mean(o * o, axis=axis, keepdims=True)
    return o * lax.rsqrt(ms + RMS_EPS) * g * (1.0 - lam_init)


def _attn_prompt_kernel(qt_ref, k_ref, vt_ref, bias_ref, lam_ref, g_ref, o_ref, *, lam_init):
    i = pl.program_id(1)
    blk = o_ref.shape[0]
    hps = qt_ref.shape[0]
    chains = [(hh, c) for hh in range(hps) for c in range(2)]

    comp = lax.broadcasted_iota(jnp.int32, (DV_A, blk), 0) // DH_A
    zero = jnp.zeros((DV_A, blk), BF16)
    qc = [jnp.where(comp == c, qt_ref[hh], zero) for hh, c in chains]

    def k_block(hh, kb):
        return k_ref[pl.ds(pl.multiple_of(kb * blk, blk), blk), hh * DV_A:(hh + 1) * DV_A]

    def attend(ks, vts, biases, carry):
        s = [_dot(ks[hh], qc[n]) for n, (hh, c) in enumerate(chains)]
        if biases is not None:
            s = [s[n] + biases[hh] for n, (hh, c) in enumerate(chains)]
        upd = [_softmax_update(s[n], carry[3 * n], carry[3 * n + 1], 0) for n in range(len(chains))]
        out = []
        for n, (hh, c) in enumerate(chains):
            m, l, alpha, p = upd[n]
            out.extend((m, l, alpha * carry[3 * n + 2] + _dot(vts[hh], p)))
        return tuple(out)

    def far_step(kb, carry):
        return attend([k_block(hh, kb) for hh in range(hps)], [vt_ref[hh, kb] for hh in range(hps)], None, carry)

    init = (jnp.full((1, blk), NEG, F32), jnp.zeros((1, blk), F32), jnp.zeros((DV_A, blk), F32)) * len(chains)
    carry = lax.fori_loop(0, jnp.maximum(i - 1, 0), far_step, init)

    kb0 = jnp.maximum(i - 1, 0)
    key = lax.broadcasted_iota(jnp.int32, (2 * blk, blk), 0)
    no_prev = jnp.logical_and(i == 0, key < blk)
    biases = [jnp.where(no_prev, NEG, bias_ref[hh]) for hh in range(hps)]
    k01 = [jnp.concatenate([k_block(hh, kb0), k_block(hh, i)], axis=0) for hh in range(hps)]
    vt01 = [jnp.concatenate([vt_ref[hh, kb0], vt_ref[hh, i]], axis=1) for hh in range(hps)]
    carry = attend(k01, vt01, biases, carry)
    lam = _lambda(lam_ref, lam_init)
    for hh in range(hps):
        c1, c2 = carry[6 * hh:6 * hh + 3], carry[6 * hh + 3:6 * hh + 6]
        y = _diff_finish(c1[2] / c1[1], c2[2] / c2[1], lam, g_ref[...], lam_init, 0)
        o_ref[:, hh * DV_A:(hh + 1) * DV_A] = y.T


def _attn_prompt(qt, kb, vt, bias_near, lamv, g_col, ya, t_prompt, n_heads, lam_init):
    blk = ATT_BLK
    nq = t_prompt // blk
    hps = ATT_HEADS_PER_STEP
    assert n_heads % hps == 0
    kernel = functools.partial(_attn_prompt_kernel, lam_init=lam_init)
    return pl.pallas_call(
        kernel,
        grid=(n_heads // hps, nq),
        in_specs=[
            pl.BlockSpec((hps, DV_A, blk), lambda h, i: (h, 0, i)),
            pl.BlockSpec((t_prompt, hps * DV_A), lambda h, i: (0, h)),
            pl.BlockSpec((hps, nq, DV_A, blk), lambda h, i: (h, 0, 0, 0)),
            pl.BlockSpec((hps, 2 * blk, blk), lambda h, i: (h, 0, 0)),
            pl.BlockSpec((4, DH_A), lambda h, i: (0, 0)),
            pl.BlockSpec((DV_A, 1), lambda h, i: (0, 0)),
        ],
        out_specs=pl.BlockSpec((blk, hps * DV_A), lambda h, i: (i, h)),
        out_shape=ya,
        compiler_params=_cparams(("parallel", "arbitrary")),
        name="attn_prompt",
    )(qt, kb, vt, bias_near, lamv, g_col)


def _attn_sample_kernel(pt_ref, q_ref, qg_ref, kn_ref, vn_ref, *rest, n_heads, lam_init, group):
    del pt_ref
    kc_refs, vc_refs = rest[:group], rest[group:2 * group]
    blast_ref, bnew_ref, lam_ref, g_ref, _, o_ref, wq_ref, m_ref, l_ref, acc_ref = rest[2 * group:]
    step = pl.program_id(1)
    is_last = step == pl.num_programs(1) - 1
    dec_seq = q_ref.shape[0]
    rows = 2 * dec_seq
    page = kc_refs[0].shape[2] // n_heads
    lane = lax.broadcasted_iota(jnp.int32, (dec_seq, DV_A), 1)

    @pl.when(step == 0)
    def _():
        ones = _group_ones(LANES, DH_A)
        for h in range(n_heads):
            q = q_ref[:, h * DV_A:(h + 1) * DV_A]
            qn = q * lax.rsqrt(_group_sum(q * q, ones) * (1.0 / DH_A) + RMS_EPS) * qg_ref[...] * Q_SCALE
            wq = jnp.concatenate([jnp.where(lane < DH_A, qn, 0.0), jnp.where(lane >= DH_A, qn, 0.0)], axis=0)
            wq_ref[h] = wq.astype(BF16)
        m_ref[...] = jnp.full(m_ref.shape, NEG, F32)
        l_ref[...] = jnp.zeros(l_ref.shape, F32)
        acc_ref[...] = jnp.zeros(acc_ref.shape, F32)

    def update(ks, vs, biases):
        heads = range(n_heads)
        s = []
        for h in heads:
            wq = wq_ref[h]
            parts = [lax.dot_general(wq, k, _NT, preferred_element_type=F32) for k in ks[h]]
            parts[-1] = parts[-1] + biases[h]
            s.append(jnp.concatenate(parts, axis=1) if len(parts) > 1 else parts[0])
        upd = [_softmax_update(s[h], m_ref[h][:, 0:1], l_ref[h][:, 0:1], 1) for h in heads]
        for h in heads:
            m, l, alpha, p = upd[h]
            pv = _dot(p[:, 0:page], vs[h][0])
            for g in range(1, len(vs[h])):
                pv = pv + _dot(p[:, g * page:(g + 1) * page], vs[h][g])
            m_ref[h] = jnp.broadcast_to(m, (rows, LANES))
            l_ref[h] = jnp.broadcast_to(l, (rows, LANES))
            acc_ref[h] = alpha * acc_ref[h] + pv

    head_rows = lambda ref, h: ref[0, 0, pl.ds(h, page, stride=n_heads), :].astype(BF16)
    update([[head_rows(r, h) for r in kc_refs] for h in range(n_heads)],
           [[head_rows(r, h) for r in vc_refs] for h in range(n_heads)],
           [jnp.where(is_last, blast_ref[h], 0.0) for h in range(n_heads)])

    @pl.when(is_last)
    def _():
        lam = _lambda(lam_ref, lam_init)
        pad = jnp.zeros((page - dec_seq, DV_A), F32)
        new = lambda ref, h: [jnp.concatenate([ref[:, h * DV_A:(h + 1) * DV_A], pad], axis=0).astype(BF16)]
        update([new(kn_ref, h) for h in range(n_heads)], [new(vn_ref, h) for h in range(n_heads)],
               [bnew_ref[h] for h in range(n_heads)])
        for h in range(n_heads):
            o = acc_ref[h] / l_ref[h][:, 0:1]
            o_ref[:, h * DV_A:(h + 1) * DV_A] = _diff_finish(o[0:dec_seq], o[dec_seq:rows], lam, g_ref[...],
                                                             lam_init, 1)


def _attn_sample(page_table, p, qg, kn, cache_k, cache_v, layer, blast, bnew, lamv, g, ya,
                 t_prompt, dec_seq, n_heads, lam_init):
    n_seq, n_pages = page_table.shape
    depth, n_pool, page = cache_k.shape[:3]
    assert page == LANES
    group = max(g for g in (8, 4, 2, 1) if n_pages % g == 0)
    w_a = n_heads * DV_A
    rows = 2 * dec_seq
    row0 = t_prompt // dec_seq
    ck = cache_k.reshape(depth, n_pool, page * n_heads, DV_A)
    cv = cache_v.reshape(depth, n_pool, page * n_heads, DV_A)
    new_rows = lambda col: (lambda b, s, pt: (row0 + b, col))
    const = lambda *shape: pl.BlockSpec(shape, lambda b, s, pt: (0,) * len(shape))
    cache_spec = lambda g: pl.BlockSpec((1, 1, page * n_heads, DV_A),
                                        lambda b, s, pt: (layer, pt[b, s * group + g], 0, 0))
    kernel = functools.partial(_attn_sample_kernel, n_heads=n_heads, lam_init=lam_init, group=group)
    n_in = 5 + 2 * group + 4
    return pl.pallas_call(
        kernel,
        grid_spec=pltpu.PrefetchScalarGridSpec(
            num_scalar_prefetch=1,
            grid=(n_seq, n_pages // group),
            in_specs=[
                pl.BlockSpec((dec_seq, w_a), new_rows(0)),
                const(1, DV_A),
                pl.BlockSpec((dec_seq, w_a), new_rows(0)),
                pl.BlockSpec((dec_seq, w_a), new_rows(2)),
            ] + [cache_spec(g) for g in range(group)] * 2 + [
                const(n_heads, rows, LANES),
                const(n_heads, rows, LANES),
                const(4, DH_A),
                const(1, DV_A),
                pl.BlockSpec(memory_space=pl.ANY),
            ],
            out_specs=pl.BlockSpec((dec_seq, w_a), new_rows(0)),
            scratch_shapes=[
                pltpu.VMEM((n_heads, rows, DV_A), BF16),
                pltpu.VMEM((n_heads, rows, LANES), F32),
                pltpu.VMEM((n_heads, rows, LANES), F32),
                pltpu.VMEM((n_heads, rows, DV_A), F32),
            ],
        ),
        out_shape=jax.ShapeDtypeStruct(ya.shape, ya.dtype),
        input_output_aliases={n_in: 0},
        compiler_params=_cparams(("parallel", "arbitrary")),
        name="attn_sample",
    )(page_table, p, qg, kn, p, *([ck] * group), *([cv] * group), blast, bnew, lamv, g, ya)


def _rwkv_prep_kernel(x_ref, prev_ref, ov_ref, mu_ref, w0_ref, a0_ref, kk_ref, ka_ref, rk_ref,
                      w2_ref, a2_ref, g2_ref,
                      r_ref, lw_ref, k_ref, v_ref, kkn_ref, a_ref, g_ref, bonus_ref,
                      *, n_prompt_tiles, dec_seq, w_b):
    i = pl.program_id(0)
    x = x_ref[...]
    rows = lax.broadcasted_iota(jnp.int32, x.shape, 0)
    prev = pltpu.roll(x, 1, axis=0)
    first = jnp.where(i == 0, 0.0, prev_ref[SUBLANES - 1:SUBLANES, :])
    prev = jnp.where(rows == 0, first, prev)
    seq_start = jnp.logical_and(i >= n_prompt_tiles, rows % dec_seq == 0)
    prev = jnp.where(seq_start, ov_ref[...], prev)
    xs = x + (prev - x) * mu_ref[...]

    r = xs[:, 0:w_b]
    k = xs[:, w_b:2 * w_b]
    v = xs[:, 2 * w_b:3 * w_b]
    lora = xs[:, 3 * w_b:3 * w_b + LORA_W]
    z = -(w0_ref[...] + _dot(jnp.tanh(lora).astype(BF16), w2_ref[...]))
    softplus = jnp.maximum(z, 0.0) + jnp.log(1.0 + jnp.exp(-jnp.abs(z)))
    lw_ref[...] = -jnp.exp(-softplus - 0.5)
    a = _sigmoid(a0_ref[...] + _dot(lora.astype(BF16), a2_ref[...]))
    g_ref[...] = _dot(_sigmoid(lora).astype(BF16), g2_ref[...])
    ones = _group_ones(LANES, DH_B)
    kk = k * kk_ref[...]
    k2 = k * (1.0 + (a - 1.0) * ka_ref[...])
    rkr = r * k2 * rk_ref[...]
    for c in range(w_b // LANES):
        sl = slice(c * LANES, (c + 1) * LANES)
        kc = kk[:, sl]
        nrm = jnp.sqrt(_group_sum(kc * kc, ones))
        kkn_ref[:, sl] = kc / jnp.maximum(nrm, 1e-12)
        bonus_ref[:, sl] = _group_sum(rkr[:, sl], ones) * v[:, sl]
    r_ref[...] = r
    k_ref[...] = k2
    v_ref[...] = v
    a_ref[...] = a


def _rwkv_prep(p, col_blk, ov, mu, w0, a0, kk, ka, rk, w2p, a2p, g2p, t_prompt, dec_seq):
    n = p.shape[0]
    bm = ATT_BLK
    w_b = w0.shape[1]
    width = mu.shape[1]
    n_pt = t_prompt // bm
    vec = lambda w: pl.BlockSpec((1, w), lambda i: (0, 0))
    mat = pl.BlockSpec((LORA_W, w_b), lambda i: (0, 0))
    out = pl.BlockSpec((bm, w_b), lambda i: (i, 0))
    kernel = functools.partial(_rwkv_prep_kernel, n_prompt_tiles=n_pt, dec_seq=dec_seq, w_b=w_b)
    return pl.pallas_call(
        kernel,
        grid=(n // bm,),
        in_specs=[
            pl.BlockSpec((bm, width), lambda i: (i, col_blk)),
            pl.BlockSpec((SUBLANES, width), lambda i: (jnp.maximum(i * (bm // SUBLANES) - 1, 0), col_blk)),
            pl.BlockSpec((bm, width), lambda i: (jnp.maximum(i - n_pt, 0), 0)),
            vec(width), vec(w_b), vec(w_b), vec(w_b), vec(w_b), vec(w_b), mat, mat, mat,
        ],
        out_specs=[out] * 8,
        out_shape=[jax.ShapeDtypeStruct((n, w_b), F32)] * 8,
        compiler_params=_cparams(("parallel",)),
        name="rwkv_prep",
    )(p, p, ov, mu, w0, a0, kk, ka, rk, w2p, a2p, g2p)


def _rwkv_scan_kernel(r_ref, lw_ref, k_ref, v_ref, kk_ref, a_ref, g_ref, bonus_ref, s0_ref,
                      lng_ref, lnb_ref, yb_in_ref, y_ref, sout_ref, s_ref, *, n_heads):
    del yb_in_ref
    c_idx = pl.program_id(1)
    chunk = r_ref.shape[0]

    @pl.when(c_idx == 0)
    def _():
        s_ref[...] = s0_ref[0]

    ti = lax.broadcasted_iota(jnp.int32, (chunk, chunk), 0)
    si = lax.broadcasted_iota(jnp.int32, (chunk, chunk), 1)
    lower = ti >= si
    strict = ti > si

    lw = lw_ref[...]
    cum = _dot_hi(lower.astype(F32), lw)
    g_prev = jnp.exp(cum - lw)
    g_cur = jnp.exp(cum)
    g_inv = jnp.exp(-cum)
    cum_end = cum[chunk - 1:chunk, :]
    g_end = jnp.exp(cum_end - cum)
    g_all = jnp.exp(cum_end)
    kk = kk_ref[...]
    b = kk * a_ref[...]
    k = k_ref[...]
    heads = range(n_heads)
    split = lambda x: [x[:, h * DH_B:(h + 1) * DH_B] for h in heads]
    ah, rh, bh, kh = split(-kk * g_prev), split(r_ref[...] * g_cur), split(b * g_inv), split(k * g_inv)
    be, ke, vh, ga = split(b * g_end), split(k * g_end), split(v_ref[...]), split(g_all)
    n_double = max(int(math.log2(chunk)), 1)
    mm = functools.partial(_mm, passes=SCAN_PASSES)

    l_ab = [jnp.where(strict, mm(ah[h], bh[h], _NT), 0.0) for h in heads]
    l_ak = [jnp.where(strict, mm(ah[h], kh[h], _NT), 0.0) for h in heads]
    m_rb = [jnp.where(lower, mm(rh[h], bh[h], _NT), 0.0) for h in heads]
    m_rk = [jnp.where(lower, mm(rh[h], kh[h], _NT), 0.0) for h in heads]
    s = [s_ref[h] for h in heads]
    x = [mm(ah[h], s[h], _NT) + mm(l_ak[h], vh[h]) for h in heads]
    lp = l_ab
    x = [x[h] + mm(lp[h], x[h]) for h in heads]
    for _ in range(n_double - 1):
        lp = [mm(lp[h], lp[h]) for h in heads]
        x = [x[h] + mm(lp[h], x[h]) for h in heads]
    for h in heads:
        s_new = s[h] * ga[h] + mm(x[h], be[h], _TN) + mm(vh[h], ke[h], _TN)
        s_ref[h] = s_new
        sout_ref[0, h] = s_new
    for h in heads:
        y = mm(rh[h], s[h], _NT) + mm(m_rb[h], x[h]) + mm(m_rk[h], vh[h])
        yc = y - jnp.mean(y, axis=-1, keepdims=True)
        y_ref[:, h * DH_B:(h + 1) * DH_B] = yc * lax.rsqrt(jnp.mean(yc * yc, axis=-1, keepdims=True) + LNX_EPS)

    y_ref[...] = (y_ref[...] * lng_ref[...] + lnb_ref[...] + bonus_ref[...]) * g_ref[...]


def _rwkv_scan(streams, s0, lng, lnb, yb, row0, n_seq, seq_len, chunk):
    n, w_b = streams[0].shape
    n_heads = w_b // DH_B
    n_chunks = seq_len // chunk
    blk0 = row0 // chunk
    rows = pl.BlockSpec((chunk, w_b), lambda b, c: (blk0 + b * n_chunks + c, 0))
    vec = pl.BlockSpec((1, w_b), lambda b, c: (0, 0))
    state = pl.BlockSpec((1, n_heads, DH_B, DH_B), lambda b, c: (b, 0, 0, 0))
    in_specs = [rows] * 8 + [state, vec, vec]
    args = list(streams) + [s0, lng, lnb]
    aliases = {}
    if yb is not None:
        in_specs.append(pl.BlockSpec(memory_space=pl.ANY))
        args.append(yb)
        aliases = {11: 0}
    kernel = functools.partial(_rwkv_scan_kernel, n_heads=n_heads)
    if yb is None:
        kernel = functools.partial(_scan_no_alias, kernel)
    return pl.pallas_call(
        kernel,
        grid=(n_seq, n_chunks),
        in_specs=in_specs,
        out_specs=[rows, state],
        out_shape=[jax.ShapeDtypeStruct((n, w_b), F32),
                   jax.ShapeDtypeStruct((n_seq, n_heads, DH_B, DH_B), F32)],
        scratch_shapes=[pltpu.VMEM((n_heads, DH_B, DH_B), F32)],
        input_output_aliases=aliases,
        compiler_params=_cparams(("parallel", "arbitrary")),
        name="rwkv_scan",
    )(*args)


def _scan_no_alias(kernel, *refs):
    return kernel(*refs[:11], None, *refs[11:])


def _glu(pc, w_c):
    return pc[:, 0:w_c] * _sigmoid(pc[:, w_c:2 * w_c])


def _conv_norm_act(c, b_ref, lg_ref, lb_ref):
    c = c + b_ref[...]
    mu = jnp.mean(c, axis=-1, keepdims=True)
    cc = c - mu
    var = jnp.mean(cc * cc, axis=-1, keepdims=True)
    c = cc * lax.rsqrt(var + LN_EPS) * lg_ref[...] + lb_ref[...]
    return c * _sigmoid(c)


def _conv_prompt_kernel(pc_ref, halo_ref, w_ref, b_ref, lg_ref, lb_ref, y_ref, st_ref, buf_ref, *, sub):
    i = pl.program_id(0)
    bm, w_c = y_ref.shape
    halo = halo_ref.shape[0]
    buf_ref[0:halo, :] = jnp.where(i == 0, 0.0, _glu(halo_ref[...], w_c))
    buf_ref[halo:halo + bm, :] = _glu(pc_ref[...], w_c)
    off = halo - (CONV_K - 1)
    for s in range(bm // sub):
        acc = jnp.zeros((sub, w_c), F32)
        for j in range(CONV_K):
            acc = acc + w_ref[j:j + 1, :] * buf_ref[s * sub + off + j:s * sub + off + j + sub, :]
        y_ref[s * sub:(s + 1) * sub, :] = _conv_norm_act(acc, b_ref, lg_ref, lb_ref)
    st_ref[0] = buf_ref[halo + bm - (CONV_K - 1):halo + bm, :]


def _conv_prompt(p, col_blk, w, b, lg, lb, n_rows, t_prompt):
    bm = ATT_BLK
    halo = 32
    w_c = w.shape[1]
    vec = pl.BlockSpec((1, w_c), lambda i: (0, 0))
    return pl.pallas_call(
        functools.partial(_conv_prompt_kernel, sub=64),
        grid=(t_prompt // bm,),
        in_specs=[
            pl.BlockSpec((bm, 2 * w_c), lambda i: (i, col_blk)),
            pl.BlockSpec((halo, 2 * w_c), lambda i: (jnp.maximum(i * (bm // halo) - 1, 0), col_blk)),
            pl.BlockSpec((CONV_K, w_c), lambda i: (0, 0)),
            vec, vec, vec,
        ],
        out_specs=[
            pl.BlockSpec((bm, w_c), lambda i: (i, 0)),
            pl.BlockSpec((1, CONV_K - 1, w_c), lambda i: (0, 0, 0)),
        ],
        out_shape=[
            jax.ShapeDtypeStruct((n_rows, w_c), F32),
            jax.ShapeDtypeStruct((1, CONV_K - 1, w_c), F32),
        ],
        scratch_shapes=[pltpu.VMEM((halo + bm, w_c), F32)],
        compiler_params=_cparams(("arbitrary",)),
        name="conv_prompt",
    )(p, p, w, b, lg, lb)


def _conv_sample_kernel(pc_ref, st0_ref, w_ref, b_ref, lg_ref, lb_ref, yc_in_ref, y_ref, st_ref, buf_ref):
    del yc_in_ref
    dec_seq, w_c = y_ref.shape
    hist = CONV_K - 1
    buf_ref[0:hist, :] = st0_ref[0]
    buf_ref[hist:hist + dec_seq, :] = _glu(pc_ref[...], w_c)
    acc = jnp.zeros((dec_seq, w_c), F32)
    for j in range(CONV_K):
        acc = acc + w_ref[j:j + 1, :] * buf_ref[j:j + dec_seq, :]
    y_ref[...] = _conv_norm_act(acc, b_ref, lg_ref, lb_ref)
    st_ref[0] = buf_ref[dec_seq:dec_seq + hist, :]


def _conv_sample(p, col_blk, st0, w, b, lg, lb, yc, t_prompt, dec_seq):
    n_seq = st0.shape[0]
    w_c = w.shape[1]
    hist = CONV_K - 1
    row0 = t_prompt // dec_seq
    vec = pl.BlockSpec((1, w_c), lambda s: (0, 0))
    return pl.pallas_call(
        _conv_sample_kernel,
        grid=(n_seq,),
        in_specs=[
            pl.BlockSpec((dec_seq, 2 * w_c), lambda s: (row0 + s, col_blk)),
            pl.BlockSpec((1, hist, w_c), lambda s: (s, 0, 0)),
            pl.BlockSpec((CONV_K, w_c), lambda s: (0, 0)),
            vec, vec, vec,
            pl.BlockSpec(memory_space=pl.ANY),
        ],
        out_specs=[
            pl.BlockSpec((dec_seq, w_c), lambda s: (row0 + s, 0)),
            pl.BlockSpec((1, hist, w_c), lambda s: (s, 0, 0)),
        ],
        out_shape=[
            jax.ShapeDtypeStruct(yc.shape, yc.dtype),
            jax.ShapeDtypeStruct((n_seq, hist, w_c), F32),
        ],
        scratch_shapes=[pltpu.VMEM((hist + dec_seq + 2, w_c), F32)],
        input_output_aliases={6: 0},
        compiler_params=_cparams(("parallel",)),
        name="conv_sample",
    )(p, st0, w, b, lg, lb, yc)


def _outproj_kernel(x_ref, ya_ref, yb_ref, yc_ref, wa_ref, wb_ref, wc_ref, o_ref):
    acc = x_ref[...] + _dot(ya_ref[...].astype(BF16), wa_ref[...])
    acc = acc + _dot(yb_ref[...].astype(BF16), wb_ref[...])
    o_ref[...] = acc + _dot(yc_ref[...].astype(BF16), wc_ref[...])


def _outproj(x, ya, yb, yc, wo, bm, bn):
    n, d = x.shape
    wa, wb, wc = ya.shape[1], yb.shape[1], yc.shape[1]
    row = lambda w: pl.BlockSpec((bm, w), lambda i, j: (i, 0))
    return pl.pallas_call(
        _outproj_kernel,
        grid=(n // bm, d // bn),
        in_specs=[
            pl.BlockSpec((bm, bn), lambda i, j: (i, j)),
            row(wa), row(wb), row(wc),
            pl.BlockSpec((wa, bn), lambda i, j: (0, j)),
            pl.BlockSpec((wb, bn), lambda i, j: (wa // wb, j)),
            pl.BlockSpec((wc, bn), lambda i, j: ((wa + wb) // wc, j)),
        ],
        out_specs=pl.BlockSpec((bm, bn), lambda i, j: (i, j)),
        out_shape=jax.ShapeDtypeStruct((n, d), F32),
        compiler_params=_cparams(("parallel", "arbitrary")),
        name="outproj",
    )(x, ya, yb, yc, wo, wo, wo)


def _ffn_kernel(x_ref, g_ref, wu_ref, wd_ref, o_ref, h_ref):
    f = pl.program_id(1)

    @pl.when(f == 0)
    def _():
        x = x_ref[...]
        ms = jnp.mean(x * x, axis=-1, keepdims=True)
        h_ref[...] = (x * lax.rsqrt(ms + RMS_EPS) * g_ref[...]).astype(BF16)
        o_ref[...] = x

    u = jnp.maximum(_dot(h_ref[...], wu_ref[...]), 0.0)
    o_ref[...] += _dot((u * u).astype(BF16), wd_ref[...])


def _ffn(x, g, wu, wd, bm, bf):
    n, d = x.shape
    d_ff = wu.shape[1]
    return pl.pallas_call(
        _ffn_kernel,
        grid=(n // bm, d_ff // bf),
        in_specs=[
            pl.BlockSpec((bm, d), lambda i, f: (i, 0)),
            pl.BlockSpec((1, d), lambda i, f: (0, 0)),
            pl.BlockSpec((d, bf), lambda i, f: (0, f)),
            pl.BlockSpec((bf, d), lambda i, f: (f, 0)),
        ],
        out_specs=pl.BlockSpec((bm, d), lambda i, f: (i, 0)),
        out_shape=jax.ShapeDtypeStruct((n, d), F32),
        scratch_shapes=[pltpu.VMEM((bm, d), BF16)],
        compiler_params=_cparams(("parallel", "arbitrary")),
        name="ffn",
    )(x, g, wu, wd)


def _row_tile(n, target):
    best = SUBLANES
    for t in range(SUBLANES, target + 1, SUBLANES):
        if n % t == 0:
            best = t
    return best


def _lambda_init(l):
    return 0.8 - 0.6 * math.exp(-0.3 * l)


def _pad_rows(w, start, total):
    return jnp.zeros((total, w.shape[1]), w.dtype).at[start:start + w.shape[0]].set(w)


def kernel(x_prompt, x_sample, cache_k, cache_v, state_rwkv, state_shift, state_conv, page_table, rel_bias, norm_mix_g, w_in, q_norm_g, k_norm_g, lambda_q1, lambda_k1, lambda_q2, lambda_k2, subln_g, rwkv_mu, rwkv_w0, rwkv_w2, rwkv_a0, rwkv_a2, rwkv_g2, rwkv_k_k, rwkv_k_a, rwkv_r_k, rwkv_lnx_g, rwkv_lnx_b, conv_dw_w, conv_dw_b, conv_ln_g, conv_ln_b, w_out, norm_ffn_g, w_up, w_down):
    depth = w_in.shape[0]
    b_p, t_p, d = x_prompt.shape
    n_seq, dec_seq, _ = x_sample.shape
    assert b_p == 1 and dec_seq == SUBLANES and t_p % ATT_BLK == 0 and (n_seq * dec_seq) % ATT_BLK == 0
    w_a, w_b, w_c = d // 2, d // 4, d // 4
    n_heads = w_a // DV_A
    h_b = w_b // DH_B
    shift_w = state_shift.shape[-1]
    assert shift_w == 3 * w_b + DECAY_LORA + AAA_LORA + GATE_LORA and shift_w <= 3 * w_b + LORA_W
    shift_pad = 3 * w_b + LORA_W
    page = cache_k.shape[2]
    assert page == LANES and page >= MAX_DISTANCE and ATT_BLK >= MAX_DISTANCE
    n = t_p + n_seq * dec_seq
    bm = _row_tile(n, 1024)

    x = jnp.concatenate([x_prompt.reshape(t_p, d), x_sample.reshape(n_seq * dec_seq, d)], axis=0)
    bias_near, bias_last, bias_new = _bias_tiles(rel_bias, n_heads, ATT_BLK, page, dec_seq)

    a_end, b_end = 3 * w_a, 3 * w_a + shift_w
    col_c = 3 * w_a // (2 * w_c)
    col_b = (3 * w_a + 2 * w_c) // shift_pad
    assert col_c * 2 * w_c == 3 * w_a and col_b * shift_pad == 3 * w_a + 2 * w_c

    ks_l, vs_l, kp_l, vp_l, sp_l, ss_l, shp_l, shs_l, cvp_l, cvs_l = ([] for _ in range(10))
    for l in range(depth):
        w_l = w_in[l]
        w_in_p = jnp.concatenate(
            [w_l[:, :a_end], w_l[:, b_end:], w_l[:, a_end:b_end], jnp.zeros((d, shift_pad - shift_w), w_l.dtype)],
            axis=1).astype(BF16)
        p = _inproj(x, norm_mix_g[l][None], w_in_p, bm, 1024)

        lam_init = _lambda_init(l)
        qg = jnp.tile(q_norm_g[l], 2)[None]
        kg = jnp.tile(k_norm_g[l], 2)[None]
        lamv = jnp.stack([lambda_q1[l], lambda_k1[l], lambda_q2[l], lambda_k2[l]])
        sg = subln_g[l][None]
        qt, kn, kb, vt, vf = _qkv_prep(p, qg, kg, n_heads)
        ya = _attn_prompt(qt, kb, vt, bias_near, lamv, subln_g[l][:, None], jax.ShapeDtypeStruct((n, w_a), F32),
                          t_p, n_heads, lam_init)
        ya = _attn_sample(page_table, p, qg, kn, cache_k, cache_v, l, bias_last, bias_new, lamv, sg, ya,
                          t_p, dec_seq, n_heads, lam_init)
        kp_l.append(kn[:t_p].reshape(b_p, t_p, n_heads, DV_A))
        vp_l.append(vf[:t_p].reshape(b_p, t_p, n_heads, DV_A))
        ks_l.append(kn[t_p:].reshape(n_seq, dec_seq, n_heads, DV_A))
        vs_l.append(vf[t_p:].reshape(n_seq, dec_seq, n_heads, DV_A))

        pad = ((0, 0), (0, shift_pad - shift_w))
        ov = jnp.pad(jnp.repeat(state_shift[l], dec_seq, axis=0), pad)
        lo = 0
        w2p = _pad_rows(rwkv_w2[l], lo, LORA_W).astype(BF16)
        a2p = _pad_rows(rwkv_a2[l], lo + DECAY_LORA, LORA_W).astype(BF16)
        g2p = _pad_rows(rwkv_g2[l], lo + DECAY_LORA + AAA_LORA, LORA_W).astype(BF16)
        streams = _rwkv_prep(p, col_b, ov, jnp.pad(rwkv_mu[l][None], pad), rwkv_w0[l][None], rwkv_a0[l][None],
                             rwkv_k_k[l][None], rwkv_k_a[l][None], rwkv_r_k[l].reshape(1, w_b),
                             w2p, a2p, g2p, t_p, dec_seq)
        lng, lnb = rwkv_lnx_g[l][None], rwkv_lnx_b[l][None]
        yb, s_p = _rwkv_scan(streams, jnp.zeros((b_p, h_b, DH_B, DH_B), F32), lng, lnb, None,
                             0, b_p, t_p, RW_CHUNK)
        yb, s_s = _rwkv_scan(streams, state_rwkv[l], lng, lnb, yb, t_p, n_seq, dec_seq, dec_seq)
        sp_l.append(s_p)
        ss_l.append(s_s)
        p_shift = p[:, col_b * shift_pad:col_b * shift_pad + shift_w]
        shp_l.append(p_shift[t_p - 1:t_p])
        shs_l.append(p_shift[t_p + dec_seq - 1::dec_seq])

        cw, cb = conv_dw_w[l], conv_dw_b[l][None]
        clg, clb = conv_ln_g[l][None], conv_ln_b[l][None]
        yc, cv_p = _conv_prompt(p, col_c, cw, cb, clg, clb, n, t_p)
        yc, cv_s = _conv_sample(p, col_c, state_conv[l], cw, cb, clg, clb, yc, t_p, dec_seq)
        cvp_l.append(cv_p)
        cvs_l.append(cv_s)

        x = _outproj(x, ya, yb, yc, w_out[l].astype(BF16), bm, 1024)
        x = _ffn(x, norm_ffn_g[l][None], w_up[l].astype(BF16), w_down[l].astype(BF16), bm, 512)

    y_prompt = x[:t_p].reshape(b_p, t_p, d)
    y_sample = x[t_p:].reshape(n_seq, dec_seq, d)
    return (y_prompt, y_sample, jnp.stack(kp_l), jnp.stack(vp_l), jnp.stack(ks_l), jnp.stack(vs_l),
            jnp.stack(sp_l), jnp.stack(ss_l), jnp.stack(shp_l), jnp.stack(shs_l),
            jnp.stack(cvp_l), jnp.stack(cvs_l))
```

```python
import functools
import math

import jax
import jax.numpy as jnp
from jax import lax
from jax.experimental import pallas as pl
from jax.experimental.pallas import tpu as pltpu

F32 = jnp.float32
BF16 = jnp.bfloat16

DH_A = 64
DV_A = 2 * DH_A
DH_B = 64
DECAY_LORA = 96
AAA_LORA = 96
GATE_LORA = 256
LORA_W = 512
CONV_K = 31
NUM_BUCKETS = 32
MAX_DISTANCE = 128
RMS_EPS = 1e-6
LN_EPS = 1e-5
LNX_EPS = 64e-5
NEG = -1e30
LOG2E = math.log2(math.e)
Q_SCALE = DH_A ** -0.5 * LOG2E

LANES = 128
SUBLANES = 8
VMEM_LIMIT = 56 * 1024 * 1024

ATT_BLK = 256
RW_CHUNK = 64
RW_CHUNKS_PER_STEP = 4
SCAN_PASSES = 1
ATT_HEADS_PER_STEP = 4


def _cparams(sem):
    return pltpu.CompilerParams(dimension_semantics=sem, vmem_limit_bytes=VMEM_LIMIT)


def _dot(a, b):
    return jnp.dot(a, b, preferred_element_type=F32)


def _dot_hi(a, b, dims=(((1,), (0,)), ((), ()))):
    return lax.dot_general(a, b, dims, preferred_element_type=F32, precision=lax.Precision.HIGHEST)


_NT = (((1,), (1,)), ((), ()))
_TN = (((0,), (0,)), ((), ()))
_NN = (((1,), (0,)), ((), ()))


def _mm(a, b, dims=_NN, passes=1):
    dot = lambda x, y: lax.dot_general(x, y, dims, preferred_element_type=F32)
    ah, bh = a.astype(BF16), b.astype(BF16)
    out = dot(ah, bh)
    if passes == 3:
        al = (a - ah.astype(F32)).astype(BF16)
        bl = (b - bh.astype(F32)).astype(BF16)
        out = out + dot(ah, bl) + dot(al, bh)
    return out


def _group_ones(width, group):
    r = lax.broadcasted_iota(jnp.int32, (width, width), 0) // group
    c = lax.broadcasted_iota(jnp.int32, (width, width), 1) // group
    return (r == c).astype(BF16)


def _group_sum(x, ones):
    hi = x.astype(BF16)
    lo = (x - hi.astype(F32)).astype(BF16)
    return _dot(hi, ones) + _dot(lo, ones)


def _sigmoid(x):
    return 1.0 / (1.0 + jnp.exp(-x))


def _inproj_kernel(x_ref, g_ref, w_ref, o_ref, h_ref):
    @pl.when(pl.program_id(1) == 0)
    def _():
        x = x_ref[...]
        ms = jnp.mean(x * x, axis=-1, keepdims=True)
        h_ref[...] = (x * lax.rsqrt(ms + RMS_EPS) * g_ref[...]).astype(BF16)

    o_ref[...] = _dot(h_ref[...], w_ref[...])


def _inproj(x, g, w, bm, bn):
    n, d = x.shape
    wn = w.shape[1]
    return pl.pallas_call(
        _inproj_kernel,
        grid=(n // bm, wn // bn),
        in_specs=[
            pl.BlockSpec((bm, d), lambda i, j: (i, 0)),
            pl.BlockSpec((1, d), lambda i, j: (0, 0)),
            pl.BlockSpec((d, bn), lambda i, j: (0, j)),
        ],
        out_specs=pl.BlockSpec((bm, bn), lambda i, j: (i, j)),
        out_shape=jax.ShapeDtypeStruct((n, wn), F32),
        scratch_shapes=[pltpu.VMEM((bm, d), BF16)],
        compiler_params=_cparams(("parallel", "arbitrary")),
        name="inproj",
    )(x, g, w)


def _qkv_kernel(p_ref, qg_ref, kg_ref, *rest, n_heads):
    qt_ref, kb_ref, vt_ref, kp_ref, vp_ref, ks_ref, vs_ref = rest[-7:]
    is_sample = pl.program_id(0) == 0
    ones = _group_ones(LANES, DH_A)
    w_a = n_heads * DV_A
    for h in range(n_heads):
        sl = slice(h * DV_A, (h + 1) * DV_A)
        q = p_ref[:, h * DV_A:(h + 1) * DV_A]
        qn = q * lax.rsqrt(_group_sum(q * q, ones) * (1.0 / DH_A) + RMS_EPS) * qg_ref[...]
        qt_ref[h] = (qn * Q_SCALE).T.astype(BF16)
        k = p_ref[:, w_a + h * DV_A:w_a + (h + 1) * DV_A]
        kn = k * lax.rsqrt(_group_sum(k * k, ones) * (1.0 / DH_A) + RMS_EPS) * kg_ref[...]
        kb_ref[:, sl] = kn.astype(BF16)
        v = p_ref[:, 2 * w_a + h * DV_A:2 * w_a + (h + 1) * DV_A]
        vt_ref[h, 0] = v.T.astype(BF16)

        @pl.when(is_sample)
        def _():
            ks_ref[0, :, sl] = kn
            vs_ref[0, :, sl] = v

        @pl.when(jnp.logical_not(is_sample))
        def _():
            kp_ref[0, :, sl] = kn
            vp_ref[0, :, sl] = v


def _qkv_prep(p, qg, kg, n_heads, layer, depth, t_prompt, kv_out):
    n = p.shape[0]
    bm = ATT_BLK
    w_a = n_heads * DV_A
    nb = n // bm
    nb_p = t_prompt // bm
    n_s = n - t_prompt
    assert n_s == bm
    blk = lambda i: jnp.where(i == 0, nb_p, i - 1)
    prompt_blk = lambda i: (layer, jnp.maximum(i - 1, 0), 0)
    sample_blk = lambda i: (layer, 0, 0)
    in_specs = [
        pl.BlockSpec((bm, 3 * w_a), lambda i: (blk(i), 0)),
        pl.BlockSpec((1, DV_A), lambda i: (0, 0)),
        pl.BlockSpec((1, DV_A), lambda i: (0, 0)),
    ]
    args = [p, qg, kg]
    aliases = {}
    if kv_out is not None:
        in_specs += [pl.BlockSpec(memory_space=pl.ANY)] * 4
        args += list(kv_out)
        aliases = {3 + j: 3 + j for j in range(4)}
    return pl.pallas_call(
        functools.partial(_qkv_kernel, n_heads=n_heads),
        grid=(nb,),
        in_specs=in_specs,
        out_specs=[
            pl.BlockSpec((n_heads, DV_A, bm), lambda i: (0, 0, blk(i))),
            pl.BlockSpec((bm, w_a), lambda i: (blk(i), 0)),
            pl.BlockSpec((n_heads, 1, DV_A, bm), lambda i: (0, blk(i), 0, 0)),
            pl.BlockSpec((1, bm, w_a), prompt_blk),
            pl.BlockSpec((1, bm, w_a), prompt_blk),
            pl.BlockSpec((1, bm, w_a), sample_blk),
            pl.BlockSpec((1, bm, w_a), sample_blk),
        ],
        out_shape=[
            jax.ShapeDtypeStruct((n_heads, DV_A, n), BF16),
            jax.ShapeDtypeStruct((n, w_a), BF16),
            jax.ShapeDtypeStruct((n_heads, nb, DV_A, bm), BF16),
            jax.ShapeDtypeStruct((depth, t_prompt, w_a), F32),
            jax.ShapeDtypeStruct((depth, t_prompt, w_a), F32),
            jax.ShapeDtypeStruct((depth, n_s, w_a), F32),
            jax.ShapeDtypeStruct((depth, n_s, w_a), F32),
        ],
        input_output_aliases=aliases,
        compiler_params=_cparams(("arbitrary",)),
        name="qkv_prep",
    )(*args)


def _t5_bucket(n):
    max_exact = NUM_BUCKETS // 2
    nf = jnp.maximum(n, 1).astype(F32)
    large = max_exact + (jnp.log(nf / max_exact) / math.log(MAX_DISTANCE / max_exact)
                         * (NUM_BUCKETS - max_exact)).astype(jnp.int32)
    large = jnp.minimum(large, NUM_BUCKETS - 1)
    return jnp.where(n < max_exact, n, large)


def _bias_from_distance(n, rb_ref, h, n_heads):
    bucket = _t5_bucket(jnp.maximum(n, 0))
    far = rb_ref[(NUM_BUCKETS - 1) * n_heads + h]
    val = jnp.zeros(n.shape, F32)
    for b in range(NUM_BUCKETS - 1):
        val = jnp.where(bucket == b, (rb_ref[b * n_heads + h] - far) * LOG2E, val)
    return jnp.where(n < 0, NEG, val)


def _bias_kernel(rb_ref, near_ref, last_ref, new_ref, *, n_heads, blk, page, dec_seq):
    r = lax.broadcasted_iota(jnp.int32, (2 * blk, blk), 0)
    c = lax.broadcasted_iota(jnp.int32, (2 * blk, blk), 1)
    n_near = c - r + blk
    for h in range(n_heads):
        near_ref[h] = _bias_from_distance(n_near, rb_ref, h, n_heads)
    rows = 2 * dec_seq
    n_pairs = n_heads // 2
    t = lax.broadcasted_iota(jnp.int32, (rows, 2 * page), 0) % dec_seq
    col = lax.broadcasted_iota(jnp.int32, (rows, 2 * page), 1)
    n_last = page + t - col // 2
    for p in range(n_pairs):
        for a in range(2):
            tile = _bias_from_distance(n_last, rb_ref, p + a * n_pairs, n_heads)
            last_ref[p, a * rows:(a + 1) * rows, :] = jnp.where(col % 2 == a, tile, NEG)
    t = lax.broadcasted_iota(jnp.int32, (rows, LANES), 0) % dec_seq
    j = lax.broadcasted_iota(jnp.int32, (rows, LANES), 1)
    n_new = jnp.where(j < dec_seq, t - j, -1)
    for h in range(n_heads):
        new_ref[h] = _bias_from_distance(n_new, rb_ref, h, n_heads)


def _bias_tiles(rel_bias, n_heads, blk, page, dec_seq):
    rows = 2 * dec_seq
    return pl.pallas_call(
        functools.partial(_bias_kernel, n_heads=n_heads, blk=blk, page=page, dec_seq=dec_seq),
        in_specs=[pl.BlockSpec(memory_space=pltpu.SMEM)],
        out_shape=[
            jax.ShapeDtypeStruct((n_heads, 2 * blk, blk), F32),
            jax.ShapeDtypeStruct((n_heads // 2, 2 * rows, 2 * page), F32),
            jax.ShapeDtypeStruct((n_heads, rows, LANES), F32),
        ],
        compiler_params=pltpu.CompilerParams(vmem_limit_bytes=VMEM_LIMIT),
        name="bias_tiles",
    )(rel_bias.reshape(-1))


def _lambda(lam_ref, lam_init):
    l = lam_ref[...]
    s1 = jnp.sum(l[0:1] * l[1:2], axis=-1, keepdims=True)
    s2 = jnp.sum(l[2:3] * l[3:4], axis=-1, keepdims=True)
    return jnp.exp(s1) - jnp.exp(s2) + lam_init


def _softmax_update(s, m, l, axis):
    m_new = jnp.maximum(m, jnp.max(s, axis=axis, keepdims=True))
    p = jnp.exp2(s - m_new)
    alpha = jnp.exp2(m - m_new)
    return m_new, alpha * l + jnp.sum(p, axis=axis, keepdims=True), alpha, p.astype(BF16)


def _diff_finish(o1, o2, lam, g, lam_init, axis):
    o = o1 - lam * o2
    ms = jnp.mean(o * o, axis=axis, keepdims=True)
    return o * lax.rsqrt(ms + RMS_EPS) * g * (1.0 - lam_init)


def _attn_prompt_kernel(qt_ref, k_ref, vt_ref, bias_ref, lam_ref, g_ref, o_ref, *, lam_init):
    i = pl.program_id(1)
    blk = o_ref.shape[0]
    hps = qt_ref.shape[0]
    chains = [(hh, c) for hh in range(hps) for c in range(2)]

    comp = lax.broadcasted_iota(jnp.int32, (DV_A, blk), 0) // DH_A
    zero = jnp.zeros((DV_A, blk), BF16)
    qc = [jnp.where(comp == c, qt_ref[hh], zero) for hh, c in chains]

    def k_block(hh, kb):
        return k_ref[pl.ds(pl.multiple_of(kb * blk, blk), blk), hh * DV_A:(hh + 1) * DV_A]

    def attend(ks, vts, biases, carry):
        s = [_dot(ks[hh], qc[n]) for n, (hh, c) in enumerate(chains)]
        if biases is not None:
            s = [s[n] + biases[hh] for n, (hh, c) in enumerate(chains)]
        upd = [_softmax_update(s[n], carry[3 * n], carry[3 * n + 1], 0) for n in range(len(chains))]
        out = []
        for n, (hh, c) in enumerate(chains):
            m, l, alpha, p = upd[n]
            out.extend((m, l, alpha * carry[3 * n + 2] + _dot(vts[hh], p)))
        return tuple(out)

    def far_step(kp, carry):
        k2 = [k_ref[pl.ds(pl.multiple_of(kp * 2 * blk, 2 * blk), 2 * blk), hh * DV_A:(hh + 1) * DV_A]
              for hh in range(hps)]
        vt2 = [jnp.concatenate([vt_ref[hh, 2 * kp], vt_ref[hh, 2 * kp + 1]], axis=1) for hh in range(hps)]
        return attend(k2, vt2, None, carry)

    init = (jnp.full((1, blk), NEG, F32), jnp.zeros((1, blk), F32), jnp.zeros((DV_A, blk), F32)) * len(chains)
    n_far = jnp.maximum(i - 1, 0)
    carry = lax.fori_loop(0, n_far // 2, far_step, init)

    kb_left = jnp.maximum(i - 2, 0)
    kb_prev = jnp.maximum(i - 1, 0)
    has_left = n_far % 2 == 1
    key = lax.broadcasted_iota(jnp.int32, (2 * blk, blk), 0)
    no_prev = jnp.logical_and(i == 0, key < blk)
    left_bias = jnp.where(has_left, jnp.zeros((blk, blk), F32), NEG)
    biases = [jnp.concatenate([left_bias, jnp.where(no_prev, NEG, bias_ref[hh])], axis=0) for hh in range(hps)]
    k3 = [jnp.concatenate([k_block(hh, kb_left), k_block(hh, kb_prev), k_block(hh, i)], axis=0)
          for hh in range(hps)]
    vt3 = [jnp.concatenate([vt_ref[hh, kb_left], vt_ref[hh, kb_prev], vt_ref[hh, i]], axis=1) for hh in range(hps)]
    carry = attend(k3, vt3, biases, carry)
    lam = _lambda(lam_ref, lam_init)
    for hh in range(hps):
        c1, c2 = carry[6 * hh:6 * hh + 3], carry[6 * hh + 3:6 * hh + 6]
        y = _diff_finish(c1[2] / c1[1], c2[2] / c2[1], lam, g_ref[...], lam_init, 0)
        o_ref[:, hh * DV_A:(hh + 1) * DV_A] = y.T


def _attn_prompt(qt, kb, vt, bias_near, lamv, g_col, ya, t_prompt, n_heads, lam_init):
    blk = ATT_BLK
    nq = t_prompt // blk
    hps = ATT_HEADS_PER_STEP
    assert n_heads % hps == 0
    kernel = functools.partial(_attn_prompt_kernel, lam_init=lam_init)
    return pl.pallas_call(
        kernel,
        grid=(n_heads // hps, nq),
        in_specs=[
            pl.BlockSpec((hps, DV_A, blk), lambda h, i: (h, 0, i)),
            pl.BlockSpec((t_prompt, hps * DV_A), lambda h, i: (0, h)),
            pl.BlockSpec((hps, nq, DV_A, blk), lambda h, i: (h, 0, 0, 0)),
            pl.BlockSpec((hps, 2 * blk, blk), lambda h, i: (h, 0, 0)),
            pl.BlockSpec((4, DH_A), lambda h, i: (0, 0)),
            pl.BlockSpec((DV_A, 1), lambda h, i: (0, 0)),
        ],
        out_specs=pl.BlockSpec((blk, hps * DV_A), lambda h, i: (i, h)),
        out_shape=ya,
        compiler_params=_cparams(("parallel", "arbitrary")),
        name="attn_prompt",
    )(qt, kb, vt, bias_near, lamv, g_col)


def _attn_sample_kernel(pt_ref, q_ref, qg_ref, kn_ref, vn_ref, *rest, n_heads, lam_init, group):
    del pt_ref
    kc_refs, vc_refs = rest[:group], rest[group:2 * group]
    blast_ref, bnew_ref, lam_ref, g_ref, _, o_ref, wq_ref, m_ref, l_ref, acc_ref = rest[2 * group:]
    step = pl.program_id(1)
    is_last = step == pl.num_programs(1) - 1
    dec_seq = q_ref.shape[0]
    rows = 2 * dec_seq
    n_pairs = n_heads // 2
    page = kc_refs[0].shape[2] // n_heads
    width = 2 * page
    lane = lax.broadcasted_iota(jnp.int32, (dec_seq, DV_A), 1)

    @pl.when(step == 0)
    def _():
        ones = _group_ones(LANES, DH_A)
        for h in range(n_heads):
            q = q_ref[:, h * DV_A:(h + 1) * DV_A]
            qn = q * lax.rsqrt(_group_sum(q * q, ones) * (1.0 / DH_A) + RMS_EPS) * qg_ref[...] * Q_SCALE
            wq = jnp.concatenate([jnp.where(lane < DH_A, qn, 0.0), jnp.where(lane >= DH_A, qn, 0.0)], axis=0)
            a = h // n_pairs
            wq_ref[h % n_pairs, a * rows:(a + 1) * rows, :] = wq.astype(BF16)
        m_ref[...] = jnp.full(m_ref.shape, NEG, F32)
        l_ref[...] = jnp.zeros(l_ref.shape, F32)
        acc_ref[...] = jnp.zeros(acc_ref.shape, F32)

    col = lax.broadcasted_iota(jnp.int32, (2 * rows, width), 1)
    row = lax.broadcasted_iota(jnp.int32, (2 * rows, width), 0)
    other_head = jnp.where(col % 2 == row // rows, 0.0, NEG)
    pair_rows = lambda ref, p: ref[0, 0, pl.ds(p, width, stride=n_pairs), :].astype(BF16)

    pairs = range(n_pairs)
    s = []
    for p in pairs:
        wq = wq_ref[p]
        parts = [lax.dot_general(wq, pair_rows(r, p), _NT, preferred_element_type=F32) for r in kc_refs]
        parts = [x + other_head for x in parts[:-1]] + [parts[-1] + jnp.where(is_last, blast_ref[p], other_head)]
        s.append(jnp.concatenate(parts, axis=1) if group > 1 else parts[0])
    upd = [_softmax_update(s[p], m_ref[p][:, 0:1], l_ref[p][:, 0:1], 1) for p in pairs]
    for p in pairs:
        m, l, alpha, prob = upd[p]
        pv = _dot(prob[:, 0:width], pair_rows(vc_refs[0], p))
        for g in range(1, group):
            pv = pv + _dot(prob[:, g * width:(g + 1) * width], pair_rows(vc_refs[g], p))
        m_ref[p] = jnp.broadcast_to(m, m_ref.shape[1:])
        l_ref[p] = jnp.broadcast_to(l, l_ref.shape[1:])
        acc_ref[p] = alpha * acc_ref[p] + pv

    @pl.when(is_last)
    def _():
        lam = _lambda(lam_ref, lam_init)
        pad = jnp.zeros((page - dec_seq, DV_A), F32)
        for h in range(n_heads):
            p, rs = h % n_pairs, slice((h // n_pairs) * rows, (h // n_pairs + 1) * rows)
            sl = slice(h * DV_A, (h + 1) * DV_A)
            k = jnp.concatenate([kn_ref[0][:, sl], pad], axis=0).astype(BF16)
            v = jnp.concatenate([vn_ref[:, sl], pad], axis=0).astype(BF16)
            sc = lax.dot_general(wq_ref[p, rs, :], k, _NT, preferred_element_type=F32) + bnew_ref[h]
            m, l, alpha, prob = _softmax_update(sc, m_ref[p, rs, 0:1], l_ref[p, rs, 0:1], 1)
            o = (alpha * acc_ref[p, rs, :] + _dot(prob, v)) / l
            o_ref[:, sl] = _diff_finish(o[0:dec_seq], o[dec_seq:rows], lam, g_ref[...], lam_init, 1)


def _attn_sample(page_table, p, qg, k_new, cache_k, cache_v, layer, blast, bnew, lamv, g, ya,
                 t_prompt, dec_seq, n_heads, lam_init):
    n_seq, n_pages = page_table.shape
    depth, n_pool, page = cache_k.shape[:3]
    assert page == LANES and n_heads % 2 == 0
    group = max(g for g in (8, 4, 2, 1) if n_pages % g == 0)
    w_a = n_heads * DV_A
    rows = 2 * dec_seq
    row0 = t_prompt // dec_seq
    ck = cache_k.reshape(depth, n_pool, page * n_heads, DV_A)
    cv = cache_v.reshape(depth, n_pool, page * n_heads, DV_A)
    new_rows = lambda col: (lambda b, s, pt: (row0 + b, col))
    const = lambda *shape: pl.BlockSpec(shape, lambda b, s, pt: (0,) * len(shape))
    cache_spec = lambda g: pl.BlockSpec((1, 1, page * n_heads, DV_A),
                                        lambda b, s, pt: (layer, pt[b, s * group + g], 0, 0))
    kernel = functools.partial(_attn_sample_kernel, n_heads=n_heads, lam_init=lam_init, group=group)
    n_in = 5 + 2 * group + 4
    return pl.pallas_call(
        kernel,
        grid_spec=pltpu.PrefetchScalarGridSpec(
            num_scalar_prefetch=1,
            grid=(n_seq, n_pages // group),
            in_specs=[
                pl.BlockSpec((dec_seq, w_a), new_rows(0)),
                const(1, DV_A),
                pl.BlockSpec((1, dec_seq, w_a), lambda b, s, pt: (layer, b, 0)),
                pl.BlockSpec((dec_seq, w_a), new_rows(2)),
            ] + [cache_spec(g) for g in range(group)] * 2 + [
                const(n_heads // 2, 2 * rows, 2 * page),
                const(n_heads, rows, LANES),
                const(4, DH_A),
                const(1, DV_A),
                pl.BlockSpec(memory_space=pl.ANY),
            ],
            out_specs=pl.BlockSpec((dec_seq, w_a), new_rows(0)),
            scratch_shapes=[
                pltpu.VMEM((n_heads // 2, 2 * rows, DV_A), BF16),
                pltpu.VMEM((n_heads // 2, 2 * rows, LANES), F32),
                pltpu.VMEM((n_heads // 2, 2 * rows, LANES), F32),
                pltpu.VMEM((n_heads // 2, 2 * rows, DV_A), F32),
            ],
        ),
        out_shape=jax.ShapeDtypeStruct(ya.shape, ya.dtype),
        input_output_aliases={n_in: 0},
        compiler_params=_cparams(("parallel", "arbitrary")),
        name="attn_sample",
    )(page_table, p, qg, k_new, p, *([ck] * group), *([cv] * group), blast, bnew, lamv, g, ya)


def _rwkv_prep_kernel(x_ref, prev_ref, ov_ref, mu_ref, w0_ref, a0_ref, kk_ref, ka_ref, rk_ref,
                      w2_ref, a2_ref, g2_ref,
                      r_ref, lw_ref, k_ref, v_ref, kkn_ref, a_ref, g_ref, bonus_ref,
                      *, n_prompt_tiles, dec_seq, w_b):
    i = pl.program_id(0)
    x = x_ref[...]
    rows = lax.broadcasted_iota(jnp.int32, x.shape, 0)
    prev = pltpu.roll(x, 1, axis=0)
    first = jnp.where(i == 0, 0.0, prev_ref[SUBLANES - 1:SUBLANES, :])
    prev = jnp.where(rows == 0, first, prev)
    seq_start = jnp.logical_and(i >= n_prompt_tiles, rows % dec_seq == 0)
    prev = jnp.where(seq_start, ov_ref[...], prev)
    xs = x + (prev - x) * mu_ref[...]

    r = xs[:, 0:w_b]
    k = xs[:, w_b:2 * w_b]
    v = xs[:, 2 * w_b:3 * w_b]
    lora = xs[:, 3 * w_b:3 * w_b + LORA_W]
    z = -(w0_ref[...] + _dot(jnp.tanh(lora).astype(BF16), w2_ref[...]))
    softplus = jnp.maximum(z, 0.0) + jnp.log(1.0 + jnp.exp(-jnp.abs(z)))
    lw_ref[...] = -jnp.exp(-softplus - 0.5)
    a = _sigmoid(a0_ref[...] + _dot(lora.astype(BF16), a2_ref[...]))
    g_ref[...] = _dot(_sigmoid(lora).astype(BF16), g2_ref[...])
    ones = _group_ones(LANES, DH_B)
    kk = k * kk_ref[...]
    k2 = k * (1.0 + (a - 1.0) * ka_ref[...])
    rkr = r * k2 * rk_ref[...]
    for c in range(w_b // LANES):
        sl = slice(c * LANES, (c + 1) * LANES)
        kc = kk[:, sl]
        nrm = jnp.sqrt(_group_sum(kc * kc, ones))
        kkn_ref[:, sl] = kc / jnp.maximum(nrm, 1e-12)
        bonus_ref[:, sl] = _group_sum(rkr[:, sl], ones) * v[:, sl]
    r_ref[...] = r
    k_ref[...] = k2
    v_ref[...] = v
    a_ref[...] = a


def _rwkv_prep(p, col_blk, ov, mu, w0, a0, kk, ka, rk, w2p, a2p, g2p, t_prompt, dec_seq):
    n = p.shape[0]
    bm = ATT_BLK
    w_b = w0.shape[1]
    width = mu.shape[1]
    n_pt = t_prompt // bm
    vec = lambda w: pl.BlockSpec((1, w), lambda i: (0, 0))
    mat = pl.BlockSpec((LORA_W, w_b), lambda i: (0, 0))
    out = pl.BlockSpec((bm, w_b), lambda i: (i, 0))
    kernel = functools.partial(_rwkv_prep_kernel, n_prompt_tiles=n_pt, dec_seq=dec_seq, w_b=w_b)
    return pl.pallas_call(
        kernel,
        grid=(n // bm,),
        in_specs=[
            pl.BlockSpec((bm, width), lambda i: (i, col_blk)),
            pl.BlockSpec((SUBLANES, width), lambda i: (jnp.maximum(i * (bm // SUBLANES) - 1, 0), col_blk)),
            pl.BlockSpec((bm, width), lambda i: (jnp.maximum(i - n_pt, 0), 0)),
            vec(width), vec(w_b), vec(w_b), vec(w_b), vec(w_b), vec(w_b), mat, mat, mat,
        ],
        out_specs=[out] * 8,
        out_shape=[jax.ShapeDtypeStruct((n, w_b), F32)] * 8,
        compiler_params=_cparams(("parallel",)),
        name="rwkv_prep",
    )(p, p, ov, mu, w0, a0, kk, ka, rk, w2p, a2p, g2p)


def _rwkv_scan_kernel(r_ref, lw_ref, k_ref, v_ref, kk_ref, a_ref, g_ref, bonus_ref, s0_ref,
                      lng_ref, lnb_ref, yb_in_ref, y_ref, sout_ref, s_ref, *, n_heads, chunk):
    del yb_in_ref
    c_idx = pl.program_id(1)
    n_sub = r_ref.shape[0] // chunk

    @pl.when(c_idx == 0)
    def _():
        s_ref[...] = s0_ref[0]

    ti = lax.broadcasted_iota(jnp.int32, (chunk, chunk), 0)
    si = lax.broadcasted_iota(jnp.int32, (chunk, chunk), 1)
    lower = ti >= si
    strict = ti > si
    eye = (ti == si).astype(F32)
    heads = range(n_heads)
    split = lambda x: [x[:, h * DH_B:(h + 1) * DH_B] for h in heads]
    n_double = max(int(math.log2(chunk)), 1)
    mm = functools.partial(_mm, passes=SCAN_PASSES)

    ah, rh, bh, kh, be, ke, vh, ga = ({} for _ in range(8))
    for j in range(n_sub):
        rows = slice(j * chunk, (j + 1) * chunk)
        lw = lw_ref[rows, :]
        cum = _dot_hi(lower.astype(F32), lw)
        cum_end = cum[chunk - 1:chunk, :]
        g_inv = jnp.exp(-cum)
        g_end = jnp.exp(cum_end - cum)
        kk = kk_ref[rows, :]
        b = kk * a_ref[rows, :]
        k = k_ref[rows, :]
        for name, val in ((ah, -kk * jnp.exp(cum - lw)), (rh, r_ref[rows, :] * jnp.exp(cum)), (bh, b * g_inv),
                          (kh, k * g_inv), (be, b * g_end), (ke, k * g_end), (vh, v_ref[rows, :]),
                          (ga, jnp.exp(cum_end))):
            for h, x in enumerate(split(val)):
                name[j, h] = x
    units = [(j, h) for j in range(n_sub) for h in heads]
    l_ab = {u: jnp.where(strict, mm(ah[u], bh[u], _NT), 0.0) for u in units}
    l_ak = {u: jnp.where(strict, mm(ah[u], kh[u], _NT), 0.0) for u in units}
    m_rb = {u: jnp.where(lower, mm(rh[u], bh[u], _NT), 0.0) for u in units}
    m_rk = {u: jnp.where(lower, mm(rh[u], kh[u], _NT), 0.0) for u in units}
    lp = l_ab
    t_inv = {u: eye + l_ab[u] for u in units}
    for _ in range(n_double - 1):
        lp = {u: mm(lp[u], lp[u]) for u in units}
        t_inv = {u: t_inv[u] + mm(t_inv[u], lp[u]) for u in units}
    w = {u: mm(l_ak[u], vh[u]) for u in units}
    p1 = {u: mm(t_inv[u], ah[u]) for u in units}
    u0 = {u: mm(t_inv[u], w[u]) for u in units}
    y0 = {u: mm(m_rk[u], vh[u]) for u in units}
    h0 = {u: mm(vh[u], ke[u], _TN) for u in units}

    s = [s_ref[h] for h in heads]
    for j in range(n_sub):
        u = [mm(p1[j, h], s[h], _NT) + u0[j, h] for h in heads]
        y = [mm(rh[j, h], s[h], _NT) + mm(m_rb[j, h], u[h]) + y0[j, h] for h in heads]
        s = [s[h] * ga[j, h] + mm(u[h], be[j, h], _TN) + h0[j, h] for h in heads]
        for h in heads:
            yc = y[h] - jnp.mean(y[h], axis=-1, keepdims=True)
            y_ref[j * chunk:(j + 1) * chunk, h * DH_B:(h + 1) * DH_B] = (
                yc * lax.rsqrt(jnp.mean(yc * yc, axis=-1, keepdims=True) + LNX_EPS))
    for h in heads:
        s_ref[h] = s[h]
        sout_ref[0, h] = s[h]

    y_ref[...] = (y_ref[...] * lng_ref[...] + lnb_ref[...] + bonus_ref[...]) * g_ref[...]


def _rwkv_scan(streams, s0, lng, lnb, yb, row0, n_seq, seq_len, chunk, n_sub):
    n, w_b = streams[0].shape
    n_heads = w_b // DH_B
    blk_rows = chunk * n_sub
    n_steps = seq_len // blk_rows
    blk0 = row0 // blk_rows
    rows = pl.BlockSpec((blk_rows, w_b), lambda b, c: (blk0 + b * n_steps + c, 0))
    vec = pl.BlockSpec((1, w_b), lambda b, c: (0, 0))
    state = pl.BlockSpec((1, n_heads, DH_B, DH_B), lambda b, c: (b, 0, 0, 0))
    in_specs = [rows] * 8 + [state, vec, vec]
    args = list(streams) + [s0, lng, lnb]
    aliases = {}
    if yb is not None:
        in_specs.append(pl.BlockSpec(memory_space=pl.ANY))
        args.append(yb)
        aliases = {11: 0}
    kernel = functools.partial(_rwkv_scan_kernel, n_heads=n_heads, chunk=chunk)
    if yb is None:
        kernel = functools.partial(_scan_no_alias, kernel)
    return pl.pallas_call(
        kernel,
        grid=(n_seq, n_steps),
        in_specs=in_specs,
        out_specs=[rows, state],
        out_shape=[jax.ShapeDtypeStruct((n, w_b), F32),
                   jax.ShapeDtypeStruct((n_seq, n_heads, DH_B, DH_B), F32)],
        scratch_shapes=[pltpu.VMEM((n_heads, DH_B, DH_B), F32)],
        input_output_aliases=aliases,
        compiler_params=_cparams(("parallel", "arbitrary")),
        name="rwkv_scan",
    )(*args)


def _scan_no_alias(kernel, *refs):
    return kernel(*refs[:11], None, *refs[11:])


def _glu(pc, w_c):
    return pc[:, 0:w_c] * _sigmoid(pc[:, w_c:2 * w_c])


def _conv_norm_act(c, b_ref, lg_ref, lb_ref):
    c = c + b_ref[...]
    mu = jnp.mean(c, axis=-1, keepdims=True)
    cc = c - mu
    var = jnp.mean(cc * cc, axis=-1, keepdims=True)
    c = cc * lax.rsqrt(var + LN_EPS) * lg_ref[...] + lb_ref[...]
    return c * _sigmoid(c)


def _conv_prompt_kernel(pc_ref, halo_ref, w_ref, b_ref, lg_ref, lb_ref, y_ref, st_ref, buf_ref, *, sub):
    i = pl.program_id(0)
    bm, w_c = y_ref.shape
    halo = halo_ref.shape[0]
    buf_ref[0:halo, :] = jnp.where(i == 0, 0.0, _glu(halo_ref[...], w_c))
    buf_ref[halo:halo + bm, :] = _glu(pc_ref[...], w_c)
    off = halo - (CONV_K - 1)
    for s in range(bm // sub):
        acc = jnp.zeros((sub, w_c), F32)
        for j in range(CONV_K):
            acc = acc + w_ref[j:j + 1, :] * buf_ref[s * sub + off + j:s * sub + off + j + sub, :]
        y_ref[s * sub:(s + 1) * sub, :] = _conv_norm_act(acc, b_ref, lg_ref, lb_ref)
    st_ref[0] = buf_ref[halo + bm - (CONV_K - 1):halo + bm, :]


def _conv_prompt(p, col_blk, w, b, lg, lb, n_rows, t_prompt):
    bm = ATT_BLK
    halo = 32
    w_c = w.shape[1]
    vec = pl.BlockSpec((1, w_c), lambda i: (0, 0))
    return pl.pallas_call(
        functools.partial(_conv_prompt_kernel, sub=64),
        grid=(t_prompt // bm,),
        in_specs=[
            pl.BlockSpec((bm, 2 * w_c), lambda i: (i, col_blk)),
            pl.BlockSpec((halo, 2 * w_c), lambda i: (jnp.maximum(i * (bm // halo) - 1, 0), col_blk)),
            pl.BlockSpec((CONV_K, w_c), lambda i: (0, 0)),
            vec, vec, vec,
        ],
        out_specs=[
            pl.BlockSpec((bm, w_c), lambda i: (i, 0)),
            pl.BlockSpec((1, CONV_K - 1, w_c), lambda i: (0, 0, 0)),
        ],
        out_shape=[
            jax.ShapeDtypeStruct((n_rows, w_c), F32),
            jax.ShapeDtypeStruct((1, CONV_K - 1, w_c), F32),
        ],
        scratch_shapes=[pltpu.VMEM((halo + bm, w_c), F32)],
        compiler_params=_cparams(("arbitrary",)),
        name="conv_prompt",
    )(p, p, w, b, lg, lb)


def _conv_sample_kernel(pc_ref, st0_ref, w_ref, b_ref, lg_ref, lb_ref, yc_in_ref, y_ref, st_ref, buf_ref):
    del yc_in_ref
    dec_seq, w_c = y_ref.shape
    hist = CONV_K - 1
    buf_ref[0:hist, :] = st0_ref[0]
    buf_ref[hist:hist + dec_seq, :] = _glu(pc_ref[...], w_c)
    acc = jnp.zeros((dec_seq, w_c), F32)
    for j in range(CONV_K):
        acc = acc + w_ref[j:j + 1, :] * buf_ref[j:j + dec_seq, :]
    y_ref[...] = _conv_norm_act(acc, b_ref, lg_ref, lb_ref)
    st_ref[0] = buf_ref[dec_seq:dec_seq + hist, :]


def _conv_sample(p, col_blk, st0, w, b, lg, lb, yc, t_prompt, dec_seq):
    n_seq = st0.shape[0]
    w_c = w.shape[1]
    hist = CONV_K - 1
    row0 = t_prompt // dec_seq
    vec = pl.BlockSpec((1, w_c), lambda s: (0, 0))
    return pl.pallas_call(
        _conv_sample_kernel,
        grid=(n_seq,),
        in_specs=[
            pl.BlockSpec((dec_seq, 2 * w_c), lambda s: (row0 + s, col_blk)),
            pl.BlockSpec((1, hist, w_c), lambda s: (s, 0, 0)),
            pl.BlockSpec((CONV_K, w_c), lambda s: (0, 0)),
            vec, vec, vec,
            pl.BlockSpec(memory_space=pl.ANY),
        ],
        out_specs=[
            pl.BlockSpec((dec_seq, w_c), lambda s: (row0 + s, 0)),
            pl.BlockSpec((1, hist, w_c), lambda s: (s, 0, 0)),
        ],
        out_shape=[
            jax.ShapeDtypeStruct(yc.shape, yc.dtype),
            jax.ShapeDtypeStruct((n_seq, hist, w_c), F32),
        ],
        scratch_shapes=[pltpu.VMEM((hist + dec_seq + 2, w_c), F32)],
        input_output_aliases={6: 0},
        compiler_params=_cparams(("parallel",)),
        name="conv_sample",
    )(p, st0, w, b, lg, lb, yc)


def _outproj_kernel(x_ref, ya_ref, yb_ref, yc_ref, wa_ref, wb_ref, wc_ref, o_ref):
    acc = x_ref[...] + _dot(ya_ref[...].astype(BF16), wa_ref[...])
    acc = acc + _dot(yb_ref[...].astype(BF16), wb_ref[...])
    o_ref[...] = acc + _dot(yc_ref[...].astype(BF16), wc_ref[...])


def _outproj(x, ya, yb, yc, wo, bm, bn):
    n, d = x.shape
    wa, wb, wc = ya.shape[1], yb.shape[1], yc.shape[1]
    row = lambda w: pl.BlockSpec((bm, w), lambda i, j: (i, 0))
    return pl.pallas_call(
        _outproj_kernel,
        grid=(n // bm, d // bn),
        in_specs=[
            pl.BlockSpec((bm, bn), lambda i, j: (i, j)),
            row(wa), row(wb), row(wc),
            pl.BlockSpec((wa, bn), lambda i, j: (0, j)),
            pl.BlockSpec((wb, bn), lambda i, j: (wa // wb, j)),
            pl.BlockSpec((wc, bn), lambda i, j: ((wa + wb) // wc, j)),
        ],
        out_specs=pl.BlockSpec((bm, bn), lambda i, j: (i, j)),
        out_shape=jax.ShapeDtypeStruct((n, d), F32),
        compiler_params=_cparams(("parallel", "arbitrary")),
        name="outproj",
    )(x, ya, yb, yc, wo, wo, wo)


def _ffn_kernel(x_ref, g_ref, wu_ref, wd_ref, o_ref, h_ref):
    f = pl.program_id(1)

    @pl.when(f == 0)
    def _():
        x = x_ref[...]
        ms = jnp.mean(x * x, axis=-1, keepdims=True)
        h_ref[...] = (x * lax.rsqrt(ms + RMS_EPS) * g_ref[...]).astype(BF16)
        o_ref[...] = x

    u = jnp.maximum(_dot(h_ref[...], wu_ref[...]), 0.0)
    o_ref[...] += _dot((u * u).astype(BF16), wd_ref[...])


def _ffn(x, g, wu, wd, bm, bf):
    n, d = x.shape
    d_ff = wu.shape[1]
    return pl.pallas_call(
        _ffn_kernel,
        grid=(n // bm, d_ff // bf),
        in_specs=[
            pl.BlockSpec((bm, d), lambda i, f: (i, 0)),
            pl.BlockSpec((1, d), lambda i, f: (0, 0)),
            pl.BlockSpec((d, bf), lambda i, f: (0, f)),
            pl.BlockSpec((bf, d), lambda i, f: (f, 0)),
        ],
        out_specs=pl.BlockSpec((bm, d), lambda i, f: (i, 0)),
        out_shape=jax.ShapeDtypeStruct((n, d), F32),
        scratch_shapes=[pltpu.VMEM((bm, d), BF16)],
        compiler_params=_cparams(("parallel", "arbitrary")),
        name="ffn",
    )(x, g, wu, wd)


def _row_tile(n, target):
    best = SUBLANES
    for t in range(SUBLANES, target + 1, SUBLANES):
        if n % t == 0:
            best = t
    return best


def _lambda_init(l):
    return 0.8 - 0.6 * math.exp(-0.3 * l)


def _pad_rows(w, start, total):
    return jnp.zeros((total, w.shape[1]), w.dtype).at[start:start + w.shape[0]].set(w)


def kernel(x_prompt, x_sample, cache_k, cache_v, state_rwkv, state_shift, state_conv, page_table, rel_bias, norm_mix_g, w_in, q_norm_g, k_norm_g, lambda_q1, lambda_k1, lambda_q2, lambda_k2, subln_g, rwkv_mu, rwkv_w0, rwkv_w2, rwkv_a0, rwkv_a2, rwkv_g2, rwkv_k_k, rwkv_k_a, rwkv_r_k, rwkv_lnx_g, rwkv_lnx_b, conv_dw_w, conv_dw_b, conv_ln_g, conv_ln_b, w_out, norm_ffn_g, w_up, w_down):
    depth = w_in.shape[0]
    b_p, t_p, d = x_prompt.shape
    n_seq, dec_seq, _ = x_sample.shape
    assert b_p == 1 and dec_seq == SUBLANES and t_p % ATT_BLK == 0 and (n_seq * dec_seq) % ATT_BLK == 0
    w_a, w_b, w_c = d // 2, d // 4, d // 4
    n_heads = w_a // DV_A
    h_b = w_b // DH_B
    shift_w = state_shift.shape[-1]
    assert shift_w == 3 * w_b + DECAY_LORA + AAA_LORA + GATE_LORA and shift_w <= 3 * w_b + LORA_W
    shift_pad = 3 * w_b + LORA_W
    page = cache_k.shape[2]
    assert page == LANES and page >= MAX_DISTANCE and ATT_BLK >= MAX_DISTANCE
    n = t_p + n_seq * dec_seq
    bm = _row_tile(n, 1024)

    x = jnp.concatenate([x_prompt.reshape(t_p, d), x_sample.reshape(n_seq * dec_seq, d)], axis=0)
    bias_near, bias_last, bias_new = _bias_tiles(rel_bias, n_heads, ATT_BLK, page, dec_seq)

    a_end, b_end = 3 * w_a, 3 * w_a + shift_w
    col_c = 3 * w_a // (2 * w_c)
    col_b = (3 * w_a + 2 * w_c) // shift_pad
    assert col_c * 2 * w_c == 3 * w_a and col_b * shift_pad == 3 * w_a + 2 * w_c

    sp_l, ss_l, shp_l, shs_l, cvp_l, cvs_l = ([] for _ in range(6))
    kv_out = None
    for l in range(depth):
        w_l = w_in[l]
        w_in_p = jnp.concatenate(
            [w_l[:, :a_end], w_l[:, b_end:], w_l[:, a_end:b_end], jnp.zeros((d, shift_pad - shift_w), w_l.dtype)],
            axis=1).astype(BF16)
        p = _inproj(x, norm_mix_g[l][None], w_in_p, bm, 1024)

        lam_init = _lambda_init(l)
        qg = jnp.tile(q_norm_g[l], 2)[None]
        kg = jnp.tile(k_norm_g[l], 2)[None]
        lamv = jnp.stack([lambda_q1[l], lambda_k1[l], lambda_q2[l], lambda_k2[l]])
        sg = subln_g[l][None]
        qt, kb, vt, *kv_out = _qkv_prep(p, qg, kg, n_heads, l, depth, t_p, kv_out)
        ya = _attn_prompt(qt, kb, vt, bias_near, lamv, subln_g[l][:, None], jax.ShapeDtypeStruct((n, w_a), F32),
                          t_p, n_heads, lam_init)
        ya = _attn_sample(page_table, p, qg, kv_out[2], cache_k, cache_v, l, bias_last, bias_new, lamv, sg, ya,
                          t_p, dec_seq, n_heads, lam_init)

        pad = ((0, 0), (0, shift_pad - shift_w))
        ov = jnp.pad(jnp.repeat(state_shift[l], dec_seq, axis=0), pad)
        lo = 0
        w2p = _pad_rows(rwkv_w2[l], lo, LORA_W).astype(BF16)
        a2p = _pad_rows(rwkv_a2[l], lo + DECAY_LORA, LORA_W).astype(BF16)
        g2p = _pad_rows(rwkv_g2[l], lo + DECAY_LORA + AAA_LORA, LORA_W).astype(BF16)
        streams = _rwkv_prep(p, col_b, ov, jnp.pad(rwkv_mu[l][None], pad), rwkv_w0[l][None], rwkv_a0[l][None],
                             rwkv_k_k[l][None], rwkv_k_a[l][None], rwkv_r_k[l].reshape(1, w_b),
                             w2p, a2p, g2p, t_p, dec_seq)
        lng, lnb = rwkv_lnx_g[l][None], rwkv_lnx_b[l][None]
        yb, s_p = _rwkv_scan(streams, jnp.zeros((b_p, h_b, DH_B, DH_B), F32), lng, lnb, None,
                             0, b_p, t_p, RW_CHUNK, RW_CHUNKS_PER_STEP)
        yb, s_s = _rwkv_scan(streams, state_rwkv[l], lng, lnb, yb, t_p, n_seq, dec_seq, dec_seq, 1)
        sp_l.append(s_p)
        ss_l.append(s_s)
        p_shift = p[:, col_b * shift_pad:col_b * shift_pad + shift_w]
        shp_l.append(p_shift[t_p - 1:t_p])
        shs_l.append(p_shift[t_p + dec_seq - 1::dec_seq])

        cw, cb = conv_dw_w[l], conv_dw_b[l][None]
        clg, clb = conv_ln_g[l][None], conv_ln_b[l][None]
        yc, cv_p = _conv_prompt(p, col_c, cw, cb, clg, clb, n, t_p)
        yc, cv_s = _conv_sample(p, col_c, state_conv[l], cw, cb, clg, clb, yc, t_p, dec_seq)
        cvp_l.append(cv_p)
        cvs_l.append(cv_s)

        x = _outproj(x, ya, yb, yc, w_out[l].astype(BF16), bm, 1024)
        x = _ffn(x, norm_ffn_g[l][None], w_up[l].astype(BF16), w_down[l].astype(BF16), bm, 1024)

    y_prompt = x[:t_p].reshape(b_p, t_p, d)
    y_sample = x[t_p:].reshape(n_seq, dec_seq, d)
    k_p, v_p, k_s, v_s = kv_out
    return (y_prompt, y_sample,
            k_p.reshape(depth, b_p, t_p, n_heads, DV_A), v_p.reshape(depth, b_p, t_p, n_heads, DV_A),
            k_s.reshape(depth, n_seq, dec_seq, n_heads, DV_A), v_s.reshape(depth, n_seq, dec_seq, n_heads, DV_A),
            jnp.stack(sp_l), jnp.stack(ss_l), jnp.stack(shp_l), jnp.stack(shs_l),
            jnp.stack(cvp_l), jnp.stack(cvs_l))
```

```python
import functools
import math

import jax
import jax.numpy as jnp
from jax import lax
from jax.experimental import pallas as pl
from jax.experimental.pallas import tpu as pltpu

F32 = jnp.float32
BF16 = jnp.bfloat16

DH_A = 64
DV_A = 2 * DH_A
DH_B = 64
DECAY_LORA = 96
AAA_LORA = 96
GATE_LORA = 256
LORA_W = 512
CONV_K = 31
NUM_BUCKETS = 32
MAX_DISTANCE = 128
RMS_EPS = 1e-6
LN_EPS = 1e-5
LNX_EPS = 64e-5
NEG = -1e30
LOG2E = math.log2(math.e)
Q_SCALE = DH_A ** -0.5 * LOG2E

LANES = 128
SUBLANES = 8
VMEM_LIMIT = 56 * 1024 * 1024

ATT_BLK = 256
RW_CHUNK = 64
RW_CHUNKS_PER_STEP = 4
SCAN_PASSES = 1
ATT_HEADS_PER_STEP = 4


def _cparams(sem):
    return pltpu.CompilerParams(dimension_semantics=sem, vmem_limit_bytes=VMEM_LIMIT)


def _dot(a, b):
    return jnp.dot(a, b, preferred_element_type=F32)


def _dot_hi(a, b, dims=(((1,), (0,)), ((), ()))):
    return lax.dot_general(a, b, dims, preferred_element_type=F32, precision=lax.Precision.HIGHEST)


_NT = (((1,), (1,)), ((), ()))
_TN = (((0,), (0,)), ((), ()))
_NN = (((1,), (0,)), ((), ()))


def _mm(a, b, dims=_NN, passes=1):
    dot = lambda x, y: lax.dot_general(x, y, dims, preferred_element_type=F32)
    ah, bh = a.astype(BF16), b.astype(BF16)
    out = dot(ah, bh)
    if passes == 3:
        al = (a - ah.astype(F32)).astype(BF16)
        bl = (b - bh.astype(F32)).astype(BF16)
        out = out + dot(ah, bl) + dot(al, bh)
    return out


def _group_ones(width, group):
    r = lax.broadcasted_iota(jnp.int32, (width, width), 0) // group
    c = lax.broadcasted_iota(jnp.int32, (width, width), 1) // group
    return (r == c).astype(BF16)


def _group_sum(x, ones):
    hi = x.astype(BF16)
    lo = (x - hi.astype(F32)).astype(BF16)
    return _dot(hi, ones) + _dot(lo, ones)


def _sigmoid(x):
    return 1.0 / (1.0 + jnp.exp(-x))


def _cast_kernel(w_ref, o_ref):
    o_ref[...] = w_ref[0].astype(BF16)


def _cast_bf16(w, layer):
    _, rows, cols = w.shape
    bm = _row_tile(rows, max(SUBLANES, (1 << 21) // cols))
    return pl.pallas_call(
        _cast_kernel,
        grid=(rows // bm,),
        in_specs=[pl.BlockSpec((1, bm, cols), lambda i: (layer, i, 0))],
        out_specs=pl.BlockSpec((bm, cols), lambda i: (i, 0)),
        out_shape=jax.ShapeDtypeStruct((rows, cols), BF16),
        compiler_params=_cparams(("parallel",)),
        name="cast_bf16",
    )(w)


def _inproj_kernel(x_ref, g_ref, w_ref, o_ref, h_ref):
    @pl.when(pl.program_id(1) == 0)
    def _():
        x = x_ref[...]
        ms = jnp.mean(x * x, axis=-1, keepdims=True)
        h_ref[...] = (x * lax.rsqrt(ms + RMS_EPS) * g_ref[...]).astype(BF16)

    o_ref[...] = _dot(h_ref[...], w_ref[...])


def _inproj(x, g, w, bm, bn):
    n, d = x.shape
    wn = w.shape[1]
    return pl.pallas_call(
        _inproj_kernel,
        grid=(n // bm, wn // bn),
        in_specs=[
            pl.BlockSpec((bm, d), lambda i, j: (i, 0)),
            pl.BlockSpec((1, d), lambda i, j: (0, 0)),
            pl.BlockSpec((d, bn), lambda i, j: (0, j)),
        ],
        out_specs=pl.BlockSpec((bm, bn), lambda i, j: (i, j)),
        out_shape=jax.ShapeDtypeStruct((n, wn), F32),
        scratch_shapes=[pltpu.VMEM((bm, d), BF16)],
        compiler_params=_cparams(("parallel", "arbitrary")),
        name="inproj",
    )(x, g, w)


def _qkv_kernel(p_ref, qg_ref, kg_ref, *rest, n_heads):
    qt_ref, kb_ref, vt_ref, kp_ref, vp_ref, ks_ref, vs_ref = rest[-7:]
    is_sample = pl.program_id(0) == 0
    ones = _group_ones(LANES, DH_A)
    w_a = n_heads * DV_A
    for h in range(n_heads):
        sl = slice(h * DV_A, (h + 1) * DV_A)
        q = p_ref[:, h * DV_A:(h + 1) * DV_A]
        qn = q * lax.rsqrt(_group_sum(q * q, ones) * (1.0 / DH_A) + RMS_EPS) * qg_ref[...]
        qt_ref[h] = (qn * Q_SCALE).T.astype(BF16)
        k = p_ref[:, w_a + h * DV_A:w_a + (h + 1) * DV_A]
        kn = k * lax.rsqrt(_group_sum(k * k, ones) * (1.0 / DH_A) + RMS_EPS) * kg_ref[...]
        kb_ref[:, sl] = kn.astype(BF16)
        v = p_ref[:, 2 * w_a + h * DV_A:2 * w_a + (h + 1) * DV_A]
        vt_ref[h, 0] = v.T.astype(BF16)

        kp_ref[0, :, sl] = kn
        vp_ref[0, :, sl] = v

    @pl.when(is_sample)
    def _():
        ks_ref[...] = kp_ref[...]
        vs_ref[...] = vp_ref[...]


def _qkv_prep(p, qg, kg, n_heads, layer, depth, t_prompt, kv_out):
    n = p.shape[0]
    bm = ATT_BLK
    w_a = n_heads * DV_A
    nb = n // bm
    nb_p = t_prompt // bm
    n_s = n - t_prompt
    assert n_s == bm
    blk = lambda i: jnp.where(i == 0, nb_p, i - 1)
    prompt_blk = lambda i: (layer, jnp.maximum(i - 1, 0), 0)
    sample_blk = lambda i: (layer, 0, 0)
    in_specs = [
        pl.BlockSpec((bm, 3 * w_a), lambda i: (blk(i), 0)),
        pl.BlockSpec((1, DV_A), lambda i: (0, 0)),
        pl.BlockSpec((1, DV_A), lambda i: (0, 0)),
    ]
    args = [p, qg, kg]
    aliases = {}
    if kv_out is not None:
        in_specs += [pl.BlockSpec(memory_space=pl.ANY)] * 4
        args += list(kv_out)
        aliases = {3 + j: 3 + j for j in range(4)}
    return pl.pallas_call(
        functools.partial(_qkv_kernel, n_heads=n_heads),
        grid=(nb,),
        in_specs=in_specs,
        out_specs=[
            pl.BlockSpec((n_heads, DV_A, bm), lambda i: (0, 0, blk(i))),
            pl.BlockSpec((bm, w_a), lambda i: (blk(i), 0)),
            pl.BlockSpec((n_heads, 1, DV_A, bm), lambda i: (0, blk(i), 0, 0)),
            pl.BlockSpec((1, bm, w_a), prompt_blk),
            pl.BlockSpec((1, bm, w_a), prompt_blk),
            pl.BlockSpec((1, bm, w_a), sample_blk),
            pl.BlockSpec((1, bm, w_a), sample_blk),
        ],
        out_shape=[
            jax.ShapeDtypeStruct((n_heads, DV_A, n), BF16),
            jax.ShapeDtypeStruct((n, w_a), BF16),
            jax.ShapeDtypeStruct((n_heads, nb, DV_A, bm), BF16),
            jax.ShapeDtypeStruct((depth, t_prompt, w_a), F32),
            jax.ShapeDtypeStruct((depth, t_prompt, w_a), F32),
            jax.ShapeDtypeStruct((depth, n_s, w_a), F32),
            jax.ShapeDtypeStruct((depth, n_s, w_a), F32),
        ],
        input_output_aliases=aliases,
        compiler_params=_cparams(("arbitrary",)),
        name="qkv_prep",
    )(*args)


def _t5_bucket(n):
    max_exact = NUM_BUCKETS // 2
    nf = jnp.maximum(n, 1).astype(F32)
    large = max_exact + (jnp.log(nf / max_exact) / math.log(MAX_DISTANCE / max_exact)
                         * (NUM_BUCKETS - max_exact)).astype(jnp.int32)
    large = jnp.minimum(large, NUM_BUCKETS - 1)
    return jnp.where(n < max_exact, n, large)


def _bias_from_distance(n, rb_ref, h, n_heads):
    bucket = _t5_bucket(jnp.maximum(n, 0))
    far = rb_ref[(NUM_BUCKETS - 1) * n_heads + h]
    val = jnp.zeros(n.shape, F32)
    for b in range(NUM_BUCKETS - 1):
        val = jnp.where(bucket == b, (rb_ref[b * n_heads + h] - far) * LOG2E, val)
    return jnp.where(n < 0, NEG, val)


def _bias_kernel(rb_ref, near_ref, last_ref, new_ref, *, n_heads, blk, page, dec_seq):
    r = lax.broadcasted_iota(jnp.int32, (2 * blk, blk), 0)
    c = lax.broadcasted_iota(jnp.int32, (2 * blk, blk), 1)
    n_near = c - r + blk
    for h in range(n_heads):
        near_ref[h] = _bias_from_distance(n_near, rb_ref, h, n_heads)
    rows = 2 * dec_seq
    n_pairs = n_heads // 2
    t = lax.broadcasted_iota(jnp.int32, (rows, 2 * page), 0) % dec_seq
    col = lax.broadcasted_iota(jnp.int32, (rows, 2 * page), 1)
    n_last = page + t - col // 2
    for p in range(n_pairs):
        for a in range(2):
            tile = _bias_from_distance(n_last, rb_ref, p + a * n_pairs, n_heads)
            last_ref[p, a * rows:(a + 1) * rows, :] = jnp.where(col % 2 == a, tile, NEG)
    t = lax.broadcasted_iota(jnp.int32, (rows, LANES), 0) % dec_seq
    j = lax.broadcasted_iota(jnp.int32, (rows, LANES), 1)
    n_new = jnp.where(j < dec_seq, t - j, -1)
    for h in range(n_heads):
        new_ref[h] = _bias_from_distance(n_new, rb_ref, h, n_heads)


def _bias_tiles(rel_bias, n_heads, blk, page, dec_seq):
    rows = 2 * dec_seq
    return pl.pallas_call(
        functools.partial(_bias_kernel, n_heads=n_heads, blk=blk, page=page, dec_seq=dec_seq),
        in_specs=[pl.BlockSpec(memory_space=pltpu.SMEM)],
        out_shape=[
            jax.ShapeDtypeStruct((n_heads, 2 * blk, blk), F32),
            jax.ShapeDtypeStruct((n_heads // 2, 2 * rows, 2 * page), F32),
            jax.ShapeDtypeStruct((n_heads, rows, LANES), F32),
        ],
        compiler_params=pltpu.CompilerParams(vmem_limit_bytes=VMEM_LIMIT),
        name="bias_tiles",
    )(rel_bias.reshape(-1))


def _lambda(lam_ref, lam_init):
    l = lam_ref[...]
    s1 = jnp.sum(l[0:1] * l[1:2], axis=-1, keepdims=True)
    s2 = jnp.sum(l[2:3] * l[3:4], axis=-1, keepdims=True)
    return jnp.exp(s1) - jnp.exp(s2) + lam_init


def _softmax_update(s, m, l, axis):
    m_new = jnp.maximum(m, jnp.max(s, axis=axis, keepdims=True))
    p = jnp.exp2(s - m_new)
    alpha = jnp.exp2(m - m_new)
    return m_new, alpha * l + jnp.sum(p, axis=axis, keepdims=True), alpha, p.astype(BF16)


def _diff_finish(o1, o2, lam, g, lam_init, axis):
    o = o1 - lam * o2
    ms = jnp.mean(o * o, axis=axis, keepdims=True)
    return o * lax.rsqrt(ms + RMS_EPS) * g * (1.0 - lam_init)


def _attn_prompt_kernel(qt_ref, k_ref, vt_ref, bias_ref, lam_ref, g_ref, o_ref, *, lam_init):
    i = pl.program_id(1)
    blk = o_ref.shape[0]
    hps = qt_ref.shape[0]
    chains = [(hh, c) for hh in range(hps) for c in range(2)]

    comp = lax.broadcasted_iota(jnp.int32, (DV_A, blk), 0) // DH_A
    zero = jnp.zeros((DV_A, blk), BF16)
    qc = [jnp.where(comp == c, qt_ref[hh], zero) for hh, c in chains]

    def k_block(hh, kb):
        return k_ref[pl.ds(pl.multiple_of(kb * blk, blk), blk), hh * DV_A:(hh + 1) * DV_A]

    def attend(ks, vts, biases, carry):
        s = [_dot(ks[hh], qc[n]) for n, (hh, c) in enumerate(chains)]
        if biases is not None:
            s = [s[n] + biases[hh] for n, (hh, c) in enumerate(chains)]
        upd = [_softmax_update(s[n], carry[3 * n], carry[3 * n + 1], 0) for n in range(len(chains))]
        out = []
        for n, (hh, c) in enumerate(chains):
            m, l, alpha, p = upd[n]
            out.extend((m, l, alpha * carry[3 * n + 2] + _dot(vts[hh], p)))
        return tuple(out)

    def far_step(kp, carry):
        k2 = [k_ref[pl.ds(pl.multiple_of(kp * 2 * blk, 2 * blk), 2 * blk), hh * DV_A:(hh + 1) * DV_A]
              for hh in range(hps)]
        vt2 = [jnp.concatenate([vt_ref[hh, 2 * kp], vt_ref[hh, 2 * kp + 1]], axis=1) for hh in range(hps)]
        return attend(k2, vt2, None, carry)

    init = (jnp.full((1, blk), NEG, F32), jnp.zeros((1, blk), F32), jnp.zeros((DV_A, blk), F32)) * len(chains)
    n_far = jnp.maximum(i - 1, 0)
    carry = lax.fori_loop(0, n_far // 2, far_step, init)

    kb_left = jnp.maximum(i - 2, 0)
    kb_prev = jnp.maximum(i - 1, 0)
    has_left = n_far % 2 == 1
    key = lax.broadcasted_iota(jnp.int32, (2 * blk, blk), 0)
    no_prev = jnp.logical_and(i == 0, key < blk)
    left_bias = jnp.where(has_left, jnp.zeros((blk, blk), F32), NEG)
    biases = [jnp.concatenate([left_bias, jnp.where(no_prev, NEG, bias_ref[hh])], axis=0) for hh in range(hps)]
    k3 = [jnp.concatenate([k_block(hh, kb_left), k_block(hh, kb_prev), k_block(hh, i)], axis=0)
          for hh in range(hps)]
    vt3 = [jnp.concatenate([vt_ref[hh, kb_left], vt_ref[hh, kb_prev], vt_ref[hh, i]], axis=1) for hh in range(hps)]
    carry = attend(k3, vt3, biases, carry)
    lam = _lambda(lam_ref, lam_init)
    for hh in range(hps):
        c1, c2 = carry[6 * hh:6 * hh + 3], carry[6 * hh + 3:6 * hh + 6]
        y = _diff_finish(c1[2] / c1[1], c2[2] / c2[1], lam, g_ref[...], lam_init, 0)
        o_ref[:, hh * DV_A:(hh + 1) * DV_A] = y.T


def _attn_prompt(qt, kb, vt, bias_near, lamv, g_col, ya, t_prompt, n_heads, lam_init):
    blk = ATT_BLK
    nq = t_prompt // blk
    hps = ATT_HEADS_PER_STEP
    assert n_heads % hps == 0
    kernel = functools.partial(_attn_prompt_kernel, lam_init=lam_init)
    return pl.pallas_call(
        kernel,
        grid=(n_heads // hps, nq),
        in_specs=[
            pl.BlockSpec((hps, DV_A, blk), lambda h, i: (h, 0, i)),
            pl.BlockSpec((t_prompt, hps * DV_A), lambda h, i: (0, h)),
            pl.BlockSpec((hps, nq, DV_A, blk), lambda h, i: (h, 0, 0, 0)),
            pl.BlockSpec((hps, 2 * blk, blk), lambda h, i: (h, 0, 0)),
            pl.BlockSpec((4, DH_A), lambda h, i: (0, 0)),
            pl.BlockSpec((DV_A, 1), lambda h, i: (0, 0)),
        ],
        out_specs=pl.BlockSpec((blk, hps * DV_A), lambda h, i: (i, h)),
        out_shape=ya,
        compiler_params=_cparams(("parallel", "arbitrary")),
        name="attn_prompt",
    )(qt, kb, vt, bias_near, lamv, g_col)


def _attn_sample_kernel(pt_ref, q_ref, qg_ref, kn_ref, vn_ref, *rest, n_heads, lam_init, group):
    del pt_ref
    kc_refs, vc_refs = rest[:group], rest[group:2 * group]
    blast_ref, bnew_ref, lam_ref, g_ref, _, o_ref, wq_ref, m_ref, l_ref, acc_ref = rest[2 * group:]
    step = pl.program_id(1)
    is_last = step == pl.num_programs(1) - 1
    dec_seq = q_ref.shape[0]
    rows = 2 * dec_seq
    n_pairs = n_heads // 2
    page = kc_refs[0].shape[2] // n_heads
    width = 2 * page
    lane = lax.broadcasted_iota(jnp.int32, (dec_seq, DV_A), 1)

    @pl.when(step == 0)
    def _():
        ones = _group_ones(LANES, DH_A)
        for h in range(n_heads):
            q = q_ref[:, h * DV_A:(h + 1) * DV_A]
            qn = q * lax.rsqrt(_group_sum(q * q, ones) * (1.0 / DH_A) + RMS_EPS) * qg_ref[...] * Q_SCALE
            wq = jnp.concatenate([jnp.where(lane < DH_A, qn, 0.0), jnp.where(lane >= DH_A, qn, 0.0)], axis=0)
            a = h // n_pairs
            wq_ref[h % n_pairs, a * rows:(a + 1) * rows, :] = wq.astype(BF16)
        m_ref[...] = jnp.full(m_ref.shape, NEG, F32)
        l_ref[...] = jnp.zeros(l_ref.shape, F32)
        acc_ref[...] = jnp.zeros(acc_ref.shape, F32)

    col = lax.broadcasted_iota(jnp.int32, (2 * rows, width), 1)
    row = lax.broadcasted_iota(jnp.int32, (2 * rows, width), 0)
    other_head = jnp.where(col % 2 == row // rows, 0.0, NEG)
    pair_rows = lambda ref, p: ref[0, 0, pl.ds(p, width, stride=n_pairs), :].astype(BF16)

    pairs = range(n_pairs)
    s = []
    for p in pairs:
        wq = wq_ref[p]
        parts = [lax.dot_general(wq, pair_rows(r, p), _NT, preferred_element_type=F32) for r in kc_refs]
        parts = [x + other_head for x in parts[:-1]] + [parts[-1] + jnp.where(is_last, blast_ref[p], other_head)]
        s.append(jnp.concatenate(parts, axis=1) if group > 1 else parts[0])
    upd = [_softmax_update(s[p], m_ref[p][:, 0:1], l_ref[p][:, 0:1], 1) for p in pairs]
    for p in pairs:
        m, l, alpha, prob = upd[p]
        pv = _dot(prob[:, 0:width], pair_rows(vc_refs[0], p))
        for g in range(1, group):
            pv = pv + _dot(prob[:, g * width:(g + 1) * width], pair_rows(vc_refs[g], p))
        m_ref[p] = jnp.broadcast_to(m, m_ref.shape[1:])
        l_ref[p] = jnp.broadcast_to(l, l_ref.shape[1:])
        acc_ref[p] = alpha * acc_ref[p] + pv

    @pl.when(is_last)
    def _():
        lam = _lambda(lam_ref, lam_init)
        pad = jnp.zeros((page - dec_seq, DV_A), F32)
        for h in range(n_heads):
            p, rs = h % n_pairs, slice((h // n_pairs) * rows, (h // n_pairs + 1) * rows)
            sl = slice(h * DV_A, (h + 1) * DV_A)
            k = jnp.concatenate([kn_ref[0][:, sl], pad], axis=0).astype(BF16)
            v = jnp.concatenate([vn_ref[:, sl], pad], axis=0).astype(BF16)
            sc = lax.dot_general(wq_ref[p, rs, :], k, _NT, preferred_element_type=F32) + bnew_ref[h]
            m, l, alpha, prob = _softmax_update(sc, m_ref[p, rs, 0:1], l_ref[p, rs, 0:1], 1)
            o = (alpha * acc_ref[p, rs, :] + _dot(prob, v)) / l
            o_ref[:, sl] = _diff_finish(o[0:dec_seq], o[dec_seq:rows], lam, g_ref[...], lam_init, 1)


def _attn_sample(page_table, p, qg, k_new, cache_k, cache_v, layer, blast, bnew, lamv, g, ya,
                 t_prompt, dec_seq, n_heads, lam_init):
    n_seq, n_pages = page_table.shape
    depth, n_pool, page = cache_k.shape[:3]
    assert page == LANES and n_heads % 2 == 0
    group = max(g for g in (8, 4, 2, 1) if n_pages % g == 0)
    w_a = n_heads * DV_A
    rows = 2 * dec_seq
    row0 = t_prompt // dec_seq
    ck = cache_k.reshape(depth, n_pool, page * n_heads, DV_A)
    cv = cache_v.reshape(depth, n_pool, page * n_heads, DV_A)
    new_rows = lambda col: (lambda b, s, pt: (row0 + b, col))
    const = lambda *shape: pl.BlockSpec(shape, lambda b, s, pt: (0,) * len(shape))
    cache_spec = lambda g: pl.BlockSpec((1, 1, page * n_heads, DV_A),
                                        lambda b, s, pt: (layer, pt[b, s * group + g], 0, 0))
    kernel = functools.partial(_attn_sample_kernel, n_heads=n_heads, lam_init=lam_init, group=group)
    n_in = 5 + 2 * group + 4
    return pl.pallas_call(
        kernel,
        grid_spec=pltpu.PrefetchScalarGridSpec(
            num_scalar_prefetch=1,
            grid=(n_seq, n_pages // group),
            in_specs=[
                pl.BlockSpec((dec_seq, w_a), new_rows(0)),
                const(1, DV_A),
                pl.BlockSpec((1, dec_seq, w_a), lambda b, s, pt: (layer, b, 0)),
                pl.BlockSpec((dec_seq, w_a), new_rows(2)),
            ] + [cache_spec(g) for g in range(group)] * 2 + [
                const(n_heads // 2, 2 * rows, 2 * page),
                const(n_heads, rows, LANES),
                const(4, DH_A),
                const(1, DV_A),
                pl.BlockSpec(memory_space=pl.ANY),
            ],
            out_specs=pl.BlockSpec((dec_seq, w_a), new_rows(0)),
            scratch_shapes=[
                pltpu.VMEM((n_heads // 2, 2 * rows, DV_A), BF16),
                pltpu.VMEM((n_heads // 2, 2 * rows, LANES), F32),
                pltpu.VMEM((n_heads // 2, 2 * rows, LANES), F32),
                pltpu.VMEM((n_heads // 2, 2 * rows, DV_A), F32),
            ],
        ),
        out_shape=jax.ShapeDtypeStruct(ya.shape, ya.dtype),
        input_output_aliases={n_in: 0},
        compiler_params=_cparams(("parallel", "arbitrary")),
        name="attn_sample",
    )(page_table, p, qg, k_new, p, *([ck] * group), *([cv] * group), blast, bnew, lamv, g, ya)


def _rwkv_prep_kernel(x_ref, prev_ref, ov_ref, mu_ref, w0_ref, a0_ref, kk_ref, ka_ref, rk_ref,
                      w2_ref, a2_ref, g2_ref,
                      r_ref, lw_ref, k_ref, v_ref, kkn_ref, a_ref, g_ref, bonus_ref,
                      *, n_prompt_tiles, dec_seq, w_b):
    i = pl.program_id(0)
    x = x_ref[...]
    rows = lax.broadcasted_iota(jnp.int32, x.shape, 0)
    prev = pltpu.roll(x, 1, axis=0)
    first = jnp.where(i == 0, 0.0, prev_ref[SUBLANES - 1:SUBLANES, :])
    prev = jnp.where(rows == 0, first, prev)
    seq_start = jnp.logical_and(i >= n_prompt_tiles, rows % dec_seq == 0)
    prev = jnp.where(seq_start, ov_ref[...], prev)
    xs = x + (prev - x) * mu_ref[...]

    r = xs[:, 0:w_b]
    k = xs[:, w_b:2 * w_b]
    v = xs[:, 2 * w_b:3 * w_b]
    lora = xs[:, 3 * w_b:3 * w_b + LORA_W]
    z = -(w0_ref[...] + _dot(jnp.tanh(lora).astype(BF16), w2_ref[...]))
    softplus = jnp.maximum(z, 0.0) + jnp.log(1.0 + jnp.exp(-jnp.abs(z)))
    lw_ref[...] = -jnp.exp(-softplus - 0.5)
    a = _sigmoid(a0_ref[...] + _dot(lora.astype(BF16), a2_ref[...]))
    g_ref[...] = _dot(_sigmoid(lora).astype(BF16), g2_ref[...])
    ones = _group_ones(LANES, DH_B)
    kk = k * kk_ref[...]
    k2 = k * (1.0 + (a - 1.0) * ka_ref[...])
    rkr = r * k2 * rk_ref[...]
    for c in range(w_b // LANES):
        sl = slice(c * LANES, (c + 1) * LANES)
        kc = kk[:, sl]
        nrm = jnp.sqrt(_group_sum(kc * kc, ones))
        kkn_ref[:, sl] = kc / jnp.maximum(nrm, 1e-12)
        bonus_ref[:, sl] = _group_sum(rkr[:, sl], ones) * v[:, sl]
    r_ref[...] = r
    k_ref[...] = k2
    v_ref[...] = v
    a_ref[...] = a


def _rwkv_prep(p, col_blk, ov, mu, w0, a0, kk, ka, rk, w2p, a2p, g2p, t_prompt, dec_seq):
    n = p.shape[0]
    bm = ATT_BLK
    w_b = w0.shape[1]
    width = mu.shape[1]
    n_pt = t_prompt // bm
    vec = lambda w: pl.BlockSpec((1, w), lambda i: (0, 0))
    mat = pl.BlockSpec((LORA_W, w_b), lambda i: (0, 0))
    out = pl.BlockSpec((bm, w_b), lambda i: (i, 0))
    kernel = functools.partial(_rwkv_prep_kernel, n_prompt_tiles=n_pt, dec_seq=dec_seq, w_b=w_b)
    return pl.pallas_call(
        kernel,
        grid=(n // bm,),
        in_specs=[
            pl.BlockSpec((bm, width), lambda i: (i, col_blk)),
            pl.BlockSpec((SUBLANES, width), lambda i: (jnp.maximum(i * (bm // SUBLANES) - 1, 0), col_blk)),
            pl.BlockSpec((bm, width), lambda i: (jnp.maximum(i - n_pt, 0), 0)),
            vec(width), vec(w_b), vec(w_b), vec(w_b), vec(w_b), vec(w_b), mat, mat, mat,
        ],
        out_specs=[out] * 8,
        out_shape=[jax.ShapeDtypeStruct((n, w_b), F32)] * 8,
        compiler_params=_cparams(("parallel",)),
        name="rwkv_prep",
    )(p, p, ov, mu, w0, a0, kk, ka, rk, w2p, a2p, g2p)


def _rwkv_scan_kernel(r_ref, lw_ref, k_ref, v_ref, kk_ref, a_ref, g_ref, bonus_ref, s0_ref,
                      lng_ref, lnb_ref, yb_in_ref, y_ref, sout_ref, s_ref, *, n_heads, chunk):
    del yb_in_ref
    c_idx = pl.program_id(1)
    n_sub = r_ref.shape[0] // chunk

    @pl.when(c_idx == 0)
    def _():
        s_ref[...] = s0_ref[0]

    ti = lax.broadcasted_iota(jnp.int32, (chunk, chunk), 0)
    si = lax.broadcasted_iota(jnp.int32, (chunk, chunk), 1)
    lower = ti >= si
    strict = ti > si
    eye = (ti == si).astype(F32)
    heads = range(n_heads)
    split = lambda x: [x[:, h * DH_B:(h + 1) * DH_B] for h in heads]
    n_double = max(int(math.log2(chunk)), 1)
    mm = functools.partial(_mm, passes=SCAN_PASSES)

    ah, rh, bh, kh, be, ke, vh, ga = ({} for _ in range(8))
    for j in range(n_sub):
        rows = slice(j * chunk, (j + 1) * chunk)
        lw = lw_ref[rows, :]
        cum = _dot_hi(lower.astype(F32), lw)
        cum_end = cum[chunk - 1:chunk, :]
        g_inv = jnp.exp(-cum)
        g_end = jnp.exp(cum_end - cum)
        kk = kk_ref[rows, :]
        b = kk * a_ref[rows, :]
        k = k_ref[rows, :]
        for name, val in ((ah, -kk * jnp.exp(cum - lw)), (rh, r_ref[rows, :] * jnp.exp(cum)), (bh, b * g_inv),
                          (kh, k * g_inv), (be, b * g_end), (ke, k * g_end), (vh, v_ref[rows, :]),
                          (ga, jnp.exp(cum_end))):
            for h, x in enumerate(split(val)):
                name[j, h] = x
    units = [(j, h) for j in range(n_sub) for h in heads]
    l_ab = {u: jnp.where(strict, mm(ah[u], bh[u], _NT), 0.0) for u in units}
    l_ak = {u: jnp.where(strict, mm(ah[u], kh[u], _NT), 0.0) for u in units}
    m_rb = {u: jnp.where(lower, mm(rh[u], bh[u], _NT), 0.0) for u in units}
    m_rk = {u: jnp.where(lower, mm(rh[u], kh[u], _NT), 0.0) for u in units}
    lp = l_ab
    t_inv = {u: eye + l_ab[u] for u in units}
    for _ in range(n_double - 1):
        lp = {u: mm(lp[u], lp[u]) for u in units}
        t_inv = {u: t_inv[u] + mm(t_inv[u], lp[u]) for u in units}
    w = {u: mm(l_ak[u], vh[u]) for u in units}
    p1 = {u: mm(t_inv[u], ah[u]) for u in units}
    u0 = {u: mm(t_inv[u], w[u]) for u in units}
    y0 = {u: mm(m_rk[u], vh[u]) for u in units}
    h0 = {u: mm(vh[u], ke[u], _TN) for u in units}

    s = [s_ref[h] for h in heads]
    for j in range(n_sub):
        u = [mm(p1[j, h], s[h], _NT) + u0[j, h] for h in heads]
        y = [mm(rh[j, h], s[h], _NT) + mm(m_rb[j, h], u[h]) + y0[j, h] for h in heads]
        s = [s[h] * ga[j, h] + mm(u[h], be[j, h], _TN) + h0[j, h] for h in heads]
        for h in heads:
            yc = y[h] - jnp.mean(y[h], axis=-1, keepdims=True)
            y_ref[j * chunk:(j + 1) * chunk, h * DH_B:(h + 1) * DH_B] = (
                yc * lax.rsqrt(jnp.mean(yc * yc, axis=-1, keepdims=True) + LNX_EPS))
    for h in heads:
        s_ref[h] = s[h]
        sout_ref[0, h] = s[h]

    y_ref[...] = (y_ref[...] * lng_ref[...] + lnb_ref[...] + bonus_ref[...]) * g_ref[...]


def _rwkv_scan(streams, s0, lng, lnb, yb, row0, n_seq, seq_len, chunk, n_sub):
    n, w_b = streams[0].shape
    n_heads = w_b // DH_B
    blk_rows = chunk * n_sub
    n_steps = seq_len // blk_rows
    blk0 = row0 // blk_rows
    rows = pl.BlockSpec((blk_rows, w_b), lambda b, c: (blk0 + b * n_steps + c, 0))
    vec = pl.BlockSpec((1, w_b), lambda b, c: (0, 0))
    state = pl.BlockSpec((1, n_heads, DH_B, DH_B), lambda b, c: (b, 0, 0, 0))
    in_specs = [rows] * 8 + [state, vec, vec]
    args = list(streams) + [s0, lng, lnb]
    aliases = {}
    if yb is not None:
        in_specs.append(pl.BlockSpec(memory_space=pl.ANY))
        args.append(yb)
        aliases = {11: 0}
    kernel = functools.partial(_rwkv_scan_kernel, n_heads=n_heads, chunk=chunk)
    if yb is None:
        kernel = functools.partial(_scan_no_alias, kernel)
    return pl.pallas_call(
        kernel,
        grid=(n_seq, n_steps),
        in_specs=in_specs,
        out_specs=[rows, state],
        out_shape=[jax.ShapeDtypeStruct((n, w_b), F32),
                   jax.ShapeDtypeStruct((n_seq, n_heads, DH_B, DH_B), F32)],
        scratch_shapes=[pltpu.VMEM((n_heads, DH_B, DH_B), F32)],
        input_output_aliases=aliases,
        compiler_params=_cparams(("parallel", "arbitrary")),
        name="rwkv_scan",
    )(*args)


def _scan_no_alias(kernel, *refs):
    return kernel(*refs[:11], None, *refs[11:])


def _glu(pc, w_c):
    return pc[:, 0:w_c] * _sigmoid(pc[:, w_c:2 * w_c])


def _conv_norm_act(c, b_ref, lg_ref, lb_ref):
    c = c + b_ref[...]
    mu = jnp.mean(c, axis=-1, keepdims=True)
    cc = c - mu
    var = jnp.mean(cc * cc, axis=-1, keepdims=True)
    c = cc * lax.rsqrt(var + LN_EPS) * lg_ref[...] + lb_ref[...]
    return c * _sigmoid(c)


def _conv_prompt_kernel(pc_ref, halo_ref, w_ref, b_ref, lg_ref, lb_ref, y_ref, st_ref, buf_ref, sh_ref, *, sub):
    i = pl.program_id(0)
    bm, w_c = y_ref.shape
    halo = halo_ref.shape[0]
    n_buf = halo + bm
    buf_ref[0:halo, :] = jnp.where(i == 0, 0.0, _glu(halo_ref[...], w_c))
    buf_ref[halo:n_buf, :] = _glu(pc_ref[...], w_c)
    buf_ref[n_buf:n_buf + SUBLANES, :] = jnp.zeros((SUBLANES, w_c), F32)
    for r in range(1, SUBLANES):
        sh_ref[r - 1] = buf_ref[r:r + n_buf, :]
    off = halo - (CONV_K - 1)
    for s in range(bm // sub):
        acc = jnp.zeros((sub, w_c), F32)
        for j in range(CONV_K):
            r = (off + j) % SUBLANES
            q = s * sub + off + j - r
            src = buf_ref[q:q + sub, :] if r == 0 else sh_ref[r - 1, q:q + sub, :]
            acc = acc + w_ref[j:j + 1, :] * src
        y_ref[s * sub:(s + 1) * sub, :] = _conv_norm_act(acc, b_ref, lg_ref, lb_ref)
    st_ref[0] = buf_ref[n_buf - (CONV_K - 1):n_buf, :]


def _conv_prompt(p, col_blk, w, b, lg, lb, n_rows, t_prompt):
    bm = ATT_BLK
    halo = 32
    w_c = w.shape[1]
    vec = pl.BlockSpec((1, w_c), lambda i: (0, 0))
    return pl.pallas_call(
        functools.partial(_conv_prompt_kernel, sub=64),
        grid=(t_prompt // bm,),
        in_specs=[
            pl.BlockSpec((bm, 2 * w_c), lambda i: (i, col_blk)),
            pl.BlockSpec((halo, 2 * w_c), lambda i: (jnp.maximum(i * (bm // halo) - 1, 0), col_blk)),
            pl.BlockSpec((CONV_K, w_c), lambda i: (0, 0)),
            vec, vec, vec,
        ],
        out_specs=[
            pl.BlockSpec((bm, w_c), lambda i: (i, 0)),
            pl.BlockSpec((1, CONV_K - 1, w_c), lambda i: (0, 0, 0)),
        ],
        out_shape=[
            jax.ShapeDtypeStruct((n_rows, w_c), F32),
            jax.ShapeDtypeStruct((1, CONV_K - 1, w_c), F32),
        ],
        scratch_shapes=[pltpu.VMEM((halo + bm + SUBLANES, w_c), F32),
                        pltpu.VMEM((SUBLANES - 1, halo + bm, w_c), F32)],
        compiler_params=_cparams(("arbitrary",)),
        name="conv_prompt",
    )(p, p, w, b, lg, lb)


def _conv_sample_kernel(pc_ref, st0_ref, w_ref, b_ref, lg_ref, lb_ref, yc_in_ref, y_ref, st_ref, buf_ref):
    del yc_in_ref
    dec_seq, w_c = y_ref.shape
    hist = CONV_K - 1
    buf_ref[0:hist, :] = st0_ref[0]
    buf_ref[hist:hist + dec_seq, :] = _glu(pc_ref[...], w_c)
    acc = jnp.zeros((dec_seq, w_c), F32)
    for j in range(CONV_K):
        acc = acc + w_ref[j:j + 1, :] * buf_ref[j:j + dec_seq, :]
    y_ref[...] = _conv_norm_act(acc, b_ref, lg_ref, lb_ref)
    st_ref[0] = buf_ref[dec_seq:dec_seq + hist, :]


def _conv_sample(p, col_blk, st0, w, b, lg, lb, yc, t_prompt, dec_seq):
    n_seq = st0.shape[0]
    w_c = w.shape[1]
    hist = CONV_K - 1
    row0 = t_prompt // dec_seq
    vec = pl.BlockSpec((1, w_c), lambda s: (0, 0))
    return pl.pallas_call(
        _conv_sample_kernel,
        grid=(n_seq,),
        in_specs=[
            pl.BlockSpec((dec_seq, 2 * w_c), lambda s: (row0 + s, col_blk)),
            pl.BlockSpec((1, hist, w_c), lambda s: (s, 0, 0)),
            pl.BlockSpec((CONV_K, w_c), lambda s: (0, 0)),
            vec, vec, vec,
            pl.BlockSpec(memory_space=pl.ANY),
        ],
        out_specs=[
            pl.BlockSpec((dec_seq, w_c), lambda s: (row0 + s, 0)),
            pl.BlockSpec((1, hist, w_c), lambda s: (s, 0, 0)),
        ],
        out_shape=[
            jax.ShapeDtypeStruct(yc.shape, yc.dtype),
            jax.ShapeDtypeStruct((n_seq, hist, w_c), F32),
        ],
        scratch_shapes=[pltpu.VMEM((hist + dec_seq + 2, w_c), F32)],
        input_output_aliases={6: 0},
        compiler_params=_cparams(("parallel",)),
        name="conv_sample",
    )(p, st0, w, b, lg, lb, yc)


def _outproj_kernel(x_ref, ya_ref, yb_ref, yc_ref, wa_ref, wb_ref, wc_ref, o_ref):
    acc = x_ref[...] + _dot(ya_ref[...].astype(BF16), wa_ref[...])
    acc = acc + _dot(yb_ref[...].astype(BF16), wb_ref[...])
    o_ref[...] = acc + _dot(yc_ref[...].astype(BF16), wc_ref[...])


def _outproj(x, ya, yb, yc, wo, bm, bn):
    n, d = x.shape
    wa, wb, wc = ya.shape[1], yb.shape[1], yc.shape[1]
    row = lambda w: pl.BlockSpec((bm, w), lambda i, j: (i, 0))
    return pl.pallas_call(
        _outproj_kernel,
        grid=(n // bm, d // bn),
        in_specs=[
            pl.BlockSpec((bm, bn), lambda i, j: (i, j)),
            row(wa), row(wb), row(wc),
            pl.BlockSpec((wa, bn), lambda i, j: (0, j)),
            pl.BlockSpec((wb, bn), lambda i, j: (wa // wb, j)),
            pl.BlockSpec((wc, bn), lambda i, j: ((wa + wb) // wc, j)),
        ],
        out_specs=pl.BlockSpec((bm, bn), lambda i, j: (i, j)),
        out_shape=jax.ShapeDtypeStruct((n, d), F32),
        compiler_params=_cparams(("parallel", "arbitrary")),
        name="outproj",
    )(x, ya, yb, yc, wo, wo, wo)


def _ffn_kernel(x_ref, g_ref, wu_ref, wd_ref, o_ref, h_ref):
    f = pl.program_id(1)

    @pl.when(f == 0)
    def _():
        x = x_ref[...]
        ms = jnp.mean(x * x, axis=-1, keepdims=True)
        h_ref[...] = (x * lax.rsqrt(ms + RMS_EPS) * g_ref[...]).astype(BF16)
        o_ref[...] = x

    u = jnp.maximum(_dot(h_ref[...], wu_ref[...]), 0.0)
    o_ref[...] += _dot((u * u).astype(BF16), wd_ref[...])


def _ffn(x, g, wu, wd, bm, bf):
    n, d = x.shape
    d_ff = wu.shape[1]
    return pl.pallas_call(
        _ffn_kernel,
        grid=(n // bm, d_ff // bf),
        in_specs=[
            pl.BlockSpec((bm, d), lambda i, f: (i, 0)),
            pl.BlockSpec((1, d), lambda i, f: (0, 0)),
            pl.BlockSpec((d, bf), lambda i, f: (0, f)),
            pl.BlockSpec((bf, d), lambda i, f: (f, 0)),
        ],
        out_specs=pl.BlockSpec((bm, d), lambda i, f: (i, 0)),
        out_shape=jax.ShapeDtypeStruct((n, d), F32),
        scratch_shapes=[pltpu.VMEM((bm, d), BF16)],
        compiler_params=_cparams(("parallel", "arbitrary")),
        name="ffn",
    )(x, g, wu, wd)


def _row_tile(n, target):
    best = SUBLANES
    for t in range(SUBLANES, target + 1, SUBLANES):
        if n % t == 0:
            best = t
    return best


def _lambda_init(l):
    return 0.8 - 0.6 * math.exp(-0.3 * l)


def _pad_rows(w, start, total):
    return jnp.zeros((total, w.shape[1]), w.dtype).at[start:start + w.shape[0]].set(w)


def kernel(x_prompt, x_sample, cache_k, cache_v, state_rwkv, state_shift, state_conv, page_table, rel_bias, norm_mix_g, w_in, q_norm_g, k_norm_g, lambda_q1, lambda_k1, lambda_q2, lambda_k2, subln_g, rwkv_mu, rwkv_w0, rwkv_w2, rwkv_a0, rwkv_a2, rwkv_g2, rwkv_k_k, rwkv_k_a, rwkv_r_k, rwkv_lnx_g, rwkv_lnx_b, conv_dw_w, conv_dw_b, conv_ln_g, conv_ln_b, w_out, norm_ffn_g, w_up, w_down):
    depth = w_in.shape[0]
    b_p, t_p, d = x_prompt.shape
    n_seq, dec_seq, _ = x_sample.shape
    assert b_p == 1 and dec_seq == SUBLANES and t_p % ATT_BLK == 0 and (n_seq * dec_seq) % ATT_BLK == 0
    w_a, w_b, w_c = d // 2, d // 4, d // 4
    n_heads = w_a // DV_A
    h_b = w_b // DH_B
    shift_w = state_shift.shape[-1]
    assert shift_w == 3 * w_b + DECAY_LORA + AAA_LORA + GATE_LORA and shift_w <= 3 * w_b + LORA_W
    shift_pad = 3 * w_b + LORA_W
    page = cache_k.shape[2]
    assert page == LANES and page >= MAX_DISTANCE and ATT_BLK >= MAX_DISTANCE
    n = t_p + n_seq * dec_seq
    bm = _row_tile(n, 1024)

    x = jnp.concatenate([x_prompt.reshape(t_p, d), x_sample.reshape(n_seq * dec_seq, d)], axis=0)
    bias_near, bias_last, bias_new = _bias_tiles(rel_bias, n_heads, ATT_BLK, page, dec_seq)

    a_end, b_end = 3 * w_a, 3 * w_a + shift_w
    col_c = 3 * w_a // (2 * w_c)
    col_b = (3 * w_a + 2 * w_c) // shift_pad
    assert col_c * 2 * w_c == 3 * w_a and col_b * shift_pad == 3 * w_a + 2 * w_c

    sp_l, ss_l, shp_l, shs_l, cvp_l, cvs_l = ([] for _ in range(6))
    kv_out = None
    for l in range(depth):
        w_l = w_in[l]
        w_in_p = jnp.concatenate(
            [w_l[:, :a_end], w_l[:, b_end:], w_l[:, a_end:b_end], jnp.zeros((d, shift_pad - shift_w), w_l.dtype)],
            axis=1).astype(BF16)
        p = _inproj(x, norm_mix_g[l][None], w_in_p, bm, 2048)

        lam_init = _lambda_init(l)
        qg = jnp.tile(q_norm_g[l], 2)[None]
        kg = jnp.tile(k_norm_g[l], 2)[None]
        lamv = jnp.stack([lambda_q1[l], lambda_k1[l], lambda_q2[l], lambda_k2[l]])
        sg = subln_g[l][None]
        qt, kb, vt, *kv_out = _qkv_prep(p, qg, kg, n_heads, l, depth, t_p, kv_out)
        ya = _attn_prompt(qt, kb, vt, bias_near, lamv, subln_g[l][:, None], jax.ShapeDtypeStruct((n, w_a), F32),
                          t_p, n_heads, lam_init)
        ya = _attn_sample(page_table, p, qg, kv_out[2], cache_k, cache_v, l, bias_last, bias_new, lamv, sg, ya,
                          t_p, dec_seq, n_heads, lam_init)

        pad = ((0, 0), (0, shift_pad - shift_w))
        ov = jnp.pad(jnp.repeat(state_shift[l], dec_seq, axis=0), pad)
        lo = 0
        w2p = _pad_rows(rwkv_w2[l], lo, LORA_W).astype(BF16)
        a2p = _pad_rows(rwkv_a2[l], lo + DECAY_LORA, LORA_W).astype(BF16)
        g2p = _pad_rows(rwkv_g2[l], lo + DECAY_LORA + AAA_LORA, LORA_W).astype(BF16)
        streams = _rwkv_prep(p, col_b, ov, jnp.pad(rwkv_mu[l][None], pad), rwkv_w0[l][None], rwkv_a0[l][None],
                             rwkv_k_k[l][None], rwkv_k_a[l][None], rwkv_r_k[l].reshape(1, w_b),
                             w2p, a2p, g2p, t_p, dec_seq)
        lng, lnb = rwkv_lnx_g[l][None], rwkv_lnx_b[l][None]
        yb, s_p = _rwkv_scan(streams, jnp.zeros((b_p, h_b, DH_B, DH_B), F32), lng, lnb, None,
                             0, b_p, t_p, RW_CHUNK, RW_CHUNKS_PER_STEP)
        yb, s_s = _rwkv_scan(streams, state_rwkv[l], lng, lnb, yb, t_p, n_seq, dec_seq, dec_seq, 1)
        sp_l.append(s_p)
        ss_l.append(s_s)
        c0 = col_b * shift_pad
        shp_l.append(p[t_p - 1:t_p, c0:c0 + shift_w])
        shs_l.append(p[t_p + dec_seq - 1::dec_seq, c0:c0 + shift_w])

        cw, cb = conv_dw_w[l], conv_dw_b[l][None]
        clg, clb = conv_ln_g[l][None], conv_ln_b[l][None]
        yc, cv_p = _conv_prompt(p, col_c, cw, cb, clg, clb, n, t_p)
        yc, cv_s = _conv_sample(p, col_c, state_conv[l], cw, cb, clg, clb, yc, t_p, dec_seq)
        cvp_l.append(cv_p)
        cvs_l.append(cv_s)

        x = _outproj(x, ya, yb, yc, _cast_bf16(w_out, l), bm, 1024)
        x = _ffn(x, norm_ffn_g[l][None], _cast_bf16(w_up, l), _cast_bf16(w_down, l), bm, 1024)

    y_prompt = x[:t_p].reshape(b_p, t_p, d)
    y_sample = x[t_p:].reshape(n_seq, dec_seq, d)
    k_p, v_p, k_s, v_s = kv_out
    return (y_prompt, y_sample,
            k_p.reshape(depth, b_p, t_p, n_heads, DV_A), v_p.reshape(depth, b_p, t_p, n_heads, DV_A),
            k_s.reshape(depth, n_seq, dec_seq, n_heads, DV_A), v_s.reshape(depth, n_seq, dec_seq, n_heads, DV_A),
            jnp.stack(sp_l), jnp.stack(ss_l), jnp.stack(shp_l), jnp.stack(shs_l),
            jnp.stack(cvp_l), jnp.stack(cvs_l))
```

```python
import functools
import math

import jax
import jax.numpy as jnp
from jax import lax
from jax.experimental import pallas as pl
from jax.experimental.pallas import tpu as pltpu

F32 = jnp.float32
BF16 = jnp.bfloat16

DH_A = 64
DV_A = 2 * DH_A
DH_B = 64
DECAY_LORA = 96
AAA_LORA = 96
GATE_LORA = 256
LORA_W = 512
CONV_K = 31
NUM_BUCKETS = 32
MAX_DISTANCE = 128
RMS_EPS = 1e-6
LN_EPS = 1e-5
LNX_EPS = 64e-5
NEG = -1e30
LOG2E = math.log2(math.e)
Q_SCALE = DH_A ** -0.5 * LOG2E

LANES = 128
SUBLANES = 8
VMEM_LIMIT = 56 * 1024 * 1024

ATT_BLK = 256
RW_CHUNK = 64
RW_CHUNKS_PER_STEP = 4
SCAN_PASSES = 1
ATT_HEADS_PER_STEP = 4


def _cparams(sem):
    return pltpu.CompilerParams(dimension_semantics=sem, vmem_limit_bytes=VMEM_LIMIT)


def _dot(a, b):
    return jnp.dot(a, b, preferred_element_type=F32)


def _dot_hi(a, b, dims=(((1,), (0,)), ((), ()))):
    return lax.dot_general(a, b, dims, preferred_element_type=F32, precision=lax.Precision.HIGHEST)


_NT = (((1,), (1,)), ((), ()))
_TN = (((0,), (0,)), ((), ()))
_NN = (((1,), (0,)), ((), ()))


def _mm(a, b, dims=_NN, passes=1):
    dot = lambda x, y: lax.dot_general(x, y, dims, preferred_element_type=F32)
    ah, bh = a.astype(BF16), b.astype(BF16)
    out = dot(ah, bh)
    if passes == 3:
        al = (a - ah.astype(F32)).astype(BF16)
        bl = (b - bh.astype(F32)).astype(BF16)
        out = out + dot(ah, bl) + dot(al, bh)
    return out


def _group_ones(width, group):
    r = lax.broadcasted_iota(jnp.int32, (width, width), 0) // group
    c = lax.broadcasted_iota(jnp.int32, (width, width), 1) // group
    return (r == c).astype(BF16)


def _group_sum(x, ones):
    hi = x.astype(BF16)
    lo = (x - hi.astype(F32)).astype(BF16)
    return _dot(hi, ones) + _dot(lo, ones)


def _sigmoid(x):
    return 1.0 / (1.0 + jnp.exp(-x))


def _cast_kernel(w_ref, o_ref):
    o_ref[...] = w_ref[0].astype(BF16)


def _cast_bf16(w, layer):
    _, rows, cols = w.shape
    bm = _row_tile(rows, max(SUBLANES, (1 << 21) // cols))
    return pl.pallas_call(
        _cast_kernel,
        grid=(rows // bm,),
        in_specs=[pl.BlockSpec((1, bm, cols), lambda i: (layer, i, 0))],
        out_specs=pl.BlockSpec((bm, cols), lambda i: (i, 0)),
        out_shape=jax.ShapeDtypeStruct((rows, cols), BF16),
        compiler_params=_cparams(("parallel",)),
        name="cast_bf16",
    )(w)


def _win_prep_kernel(w_ref, o_ref, *, a_end, b_end, pad):
    n_in = w_ref.shape[2]
    c_w = n_in - b_end
    o_ref[:, 0:a_end] = w_ref[0, :, 0:a_end].astype(BF16)
    o_ref[:, a_end:a_end + c_w] = w_ref[0, :, b_end:n_in].astype(BF16)
    o_ref[:, a_end + c_w:n_in] = w_ref[0, :, a_end:b_end].astype(BF16)
    o_ref[:, n_in:n_in + pad] = jnp.zeros((o_ref.shape[0], pad), BF16)


def _win_prep(w_in, layer, a_end, b_end, pad):
    _, rows, n_in = w_in.shape
    bm = _row_tile(rows, 256)
    return pl.pallas_call(
        functools.partial(_win_prep_kernel, a_end=a_end, b_end=b_end, pad=pad),
        grid=(rows // bm,),
        in_specs=[pl.BlockSpec((1, bm, n_in), lambda i: (layer, i, 0))],
        out_specs=pl.BlockSpec((bm, n_in + pad), lambda i: (i, 0)),
        out_shape=jax.ShapeDtypeStruct((rows, n_in + pad), BF16),
        compiler_params=_cparams(("parallel",)),
        name="win_prep",
    )(w_in)


def _inproj_kernel(x_ref, g_ref, w_ref, o_ref, h_ref):
    @pl.when(pl.program_id(1) == 0)
    def _():
        x = x_ref[...]
        ms = jnp.mean(x * x, axis=-1, keepdims=True)
        h_ref[...] = (x * lax.rsqrt(ms + RMS_EPS) * g_ref[...]).astype(BF16)

    o_ref[...] = _dot(h_ref[...], w_ref[...])


def _inproj(x, g, w, bm, bn):
    n, d = x.shape
    wn = w.shape[1]
    return pl.pallas_call(
        _inproj_kernel,
        grid=(n // bm, wn // bn),
        in_specs=[
            pl.BlockSpec((bm, d), lambda i, j: (i, 0)),
            pl.BlockSpec((1, d), lambda i, j: (0, 0)),
            pl.BlockSpec((d, bn), lambda i, j: (0, j)),
        ],
        out_specs=pl.BlockSpec((bm, bn), lambda i, j: (i, j)),
        out_shape=jax.ShapeDtypeStruct((n, wn), F32),
        scratch_shapes=[pltpu.VMEM((bm, d), BF16)],
        compiler_params=_cparams(("parallel", "arbitrary")),
        name="inproj",
    )(x, g, w)


def _qkv_kernel(p_ref, qg_ref, kg_ref, *rest, n_heads):
    qt_ref, kb_ref, vt_ref, kp_ref, vp_ref, ks_ref, vs_ref = rest[-7:]
    is_sample = pl.program_id(0) == 0
    ones = _group_ones(LANES, DH_A)
    w_a = n_heads * DV_A
    for h in range(n_heads):
        sl = slice(h * DV_A, (h + 1) * DV_A)
        q = p_ref[:, h * DV_A:(h + 1) * DV_A]
        qn = q * lax.rsqrt(_group_sum(q * q, ones) * (1.0 / DH_A) + RMS_EPS) * qg_ref[...]
        qt_ref[h] = (qn * Q_SCALE).T.astype(BF16)
        k = p_ref[:, w_a + h * DV_A:w_a + (h + 1) * DV_A]
        kn = k * lax.rsqrt(_group_sum(k * k, ones) * (1.0 / DH_A) + RMS_EPS) * kg_ref[...]
        kb_ref[:, sl] = kn.astype(BF16)
        v = p_ref[:, 2 * w_a + h * DV_A:2 * w_a + (h + 1) * DV_A]
        vt_ref[h, 0] = v.T.astype(BF16)

        kp_ref[0, :, sl] = kn
        vp_ref[0, :, sl] = v

    @pl.when(is_sample)
    def _():
        ks_ref[...] = kp_ref[...]
        vs_ref[...] = vp_ref[...]


def _qkv_prep(p, qg, kg, n_heads, layer, depth, t_prompt, kv_out):
    n = p.shape[0]
    bm = ATT_BLK
    w_a = n_heads * DV_A
    nb = n // bm
    nb_p = t_prompt // bm
    n_s = n - t_prompt
    assert n_s == bm
    blk = lambda i: jnp.where(i == 0, nb_p, i - 1)
    prompt_blk = lambda i: (layer, jnp.maximum(i - 1, 0), 0)
    sample_blk = lambda i: (layer, 0, 0)
    in_specs = [
        pl.BlockSpec((bm, 3 * w_a), lambda i: (blk(i), 0)),
        pl.BlockSpec((1, DV_A), lambda i: (0, 0)),
        pl.BlockSpec((1, DV_A), lambda i: (0, 0)),
    ]
    args = [p, qg, kg]
    aliases = {}
    if kv_out is not None:
        in_specs += [pl.BlockSpec(memory_space=pl.ANY)] * 4
        args += list(kv_out)
        aliases = {3 + j: 3 + j for j in range(4)}
    return pl.pallas_call(
        functools.partial(_qkv_kernel, n_heads=n_heads),
        grid=(nb,),
        in_specs=in_specs,
        out_specs=[
            pl.BlockSpec((n_heads, DV_A, bm), lambda i: (0, 0, blk(i))),
            pl.BlockSpec((bm, w_a), lambda i: (blk(i), 0)),
            pl.BlockSpec((n_heads, 1, DV_A, bm), lambda i: (0, blk(i), 0, 0)),
            pl.BlockSpec((1, bm, w_a), prompt_blk),
            pl.BlockSpec((1, bm, w_a), prompt_blk),
            pl.BlockSpec((1, bm, w_a), sample_blk),
            pl.BlockSpec((1, bm, w_a), sample_blk),
        ],
        out_shape=[
            jax.ShapeDtypeStruct((n_heads, DV_A, n), BF16),
            jax.ShapeDtypeStruct((n, w_a), BF16),
            jax.ShapeDtypeStruct((n_heads, nb, DV_A, bm), BF16),
            jax.ShapeDtypeStruct((depth, t_prompt, w_a), F32),
            jax.ShapeDtypeStruct((depth, t_prompt, w_a), F32),
            jax.ShapeDtypeStruct((depth, n_s, w_a), F32),
            jax.ShapeDtypeStruct((depth, n_s, w_a), F32),
        ],
        input_output_aliases=aliases,
        compiler_params=_cparams(("arbitrary",)),
        name="qkv_prep",
    )(*args)


def _t5_bucket(n):
    max_exact = NUM_BUCKETS // 2
    nf = jnp.maximum(n, 1).astype(F32)
    large = max_exact + (jnp.log(nf / max_exact) / math.log(MAX_DISTANCE / max_exact)
                         * (NUM_BUCKETS - max_exact)).astype(jnp.int32)
    large = jnp.minimum(large, NUM_BUCKETS - 1)
    return jnp.where(n < max_exact, n, large)


def _bias_from_distance(n, rb_ref, h, n_heads):
    bucket = _t5_bucket(jnp.maximum(n, 0))
    far = rb_ref[(NUM_BUCKETS - 1) * n_heads + h]
    val = jnp.zeros(n.shape, F32)
    for b in range(NUM_BUCKETS - 1):
        val = jnp.where(bucket == b, (rb_ref[b * n_heads + h] - far) * LOG2E, val)
    return jnp.where(n < 0, NEG, val)


def _bias_kernel(rb_ref, near_ref, last_ref, new_ref, *, n_heads, blk, page, dec_seq):
    r = lax.broadcasted_iota(jnp.int32, (2 * blk, blk), 0)
    c = lax.broadcasted_iota(jnp.int32, (2 * blk, blk), 1)
    n_near = c - r + blk
    for h in range(n_heads):
        near_ref[h] = _bias_from_distance(n_near, rb_ref, h, n_heads)
    rows = 2 * dec_seq
    n_pairs = n_heads // 2
    t = lax.broadcasted_iota(jnp.int32, (rows, 2 * page), 0) % dec_seq
    col = lax.broadcasted_iota(jnp.int32, (rows, 2 * page), 1)
    n_last = page + t - col // 2
    for p in range(n_pairs):
        for a in range(2):
            tile = _bias_from_distance(n_last, rb_ref, p + a * n_pairs, n_heads)
            last_ref[p, a * rows:(a + 1) * rows, :] = jnp.where(col % 2 == a, tile, NEG)
    t = lax.broadcasted_iota(jnp.int32, (rows, LANES), 0) % dec_seq
    j = lax.broadcasted_iota(jnp.int32, (rows, LANES), 1)
    n_new = jnp.where(j < dec_seq, t - j, -1)
    for h in range(n_heads):
        new_ref[h] = _bias_from_distance(n_new, rb_ref, h, n_heads)


def _bias_tiles(rel_bias, n_heads, blk, page, dec_seq):
    rows = 2 * dec_seq
    return pl.pallas_call(
        functools.partial(_bias_kernel, n_heads=n_heads, blk=blk, page=page, dec_seq=dec_seq),
        in_specs=[pl.BlockSpec(memory_space=pltpu.SMEM)],
        out_shape=[
            jax.ShapeDtypeStruct((n_heads, 2 * blk, blk), F32),
            jax.ShapeDtypeStruct((n_heads // 2, 2 * rows, 2 * page), F32),
            jax.ShapeDtypeStruct((n_heads, rows, LANES), F32),
        ],
        compiler_params=pltpu.CompilerParams(vmem_limit_bytes=VMEM_LIMIT),
        name="bias_tiles",
    )(rel_bias.reshape(-1))


def _lambda(lam_ref, lam_init):
    l = lam_ref[...]
    s1 = jnp.sum(l[0:1] * l[1:2], axis=-1, keepdims=True)
    s2 = jnp.sum(l[2:3] * l[3:4], axis=-1, keepdims=True)
    return jnp.exp(s1) - jnp.exp(s2) + lam_init


def _softmax_update(s, m, l, axis):
    m_new = jnp.maximum(m, jnp.max(s, axis=axis, keepdims=True))
    p = jnp.exp2(s - m_new)
    alpha = jnp.exp2(m - m_new)
    return m_new, alpha * l + jnp.sum(p, axis=axis, keepdims=True), alpha, p.astype(BF16)


def _diff_finish(o1, o2, lam, g, lam_init, axis):
    o = o1 - lam * o2
    ms = jnp.mean(o * o, axis=axis, keepdims=True)
    return o * lax.rsqrt(ms + RMS_EPS) * g * (1.0 - lam_init)


def _attn_prompt_kernel(qt_ref, k_ref, vt_ref, bias_ref, lam_ref, g_ref, o_ref, *, lam_init):
    i = pl.program_id(1)
    blk = o_ref.shape[0]
    hps = qt_ref.shape[0]
    chains = [(hh, c) for hh in range(hps) for c in range(2)]

    comp = lax.broadcasted_iota(jnp.int32, (DV_A, blk), 0) // DH_A
    zero = jnp.zeros((DV_A, blk), BF16)
    qc = [jnp.where(comp == c, qt_ref[hh], zero) for hh, c in chains]

    def k_block(hh, kb):
        return k_ref[pl.ds(pl.multiple_of(kb * blk, blk), blk), hh * DV_A:(hh + 1) * DV_A]

    def attend(ks, vts, biases, carry):
        s = [_dot(ks[hh], qc[n]) for n, (hh, c) in enumerate(chains)]
        if biases is not None:
            s = [s[n] + biases[hh] for n, (hh, c) in enumerate(chains)]
        upd = [_softmax_update(s[n], carry[3 * n], carry[3 * n + 1], 0) for n in range(len(chains))]
        out = []
        for n, (hh, c) in enumerate(chains):
            m, l, alpha, p = upd[n]
            out.extend((m, l, alpha * carry[3 * n + 2] + _dot(vts[hh], p)))
        return tuple(out)

    def far_step(kp, carry):
        k2 = [k_ref[pl.ds(pl.multiple_of(kp * 2 * blk, 2 * blk), 2 * blk), hh * DV_A:(hh + 1) * DV_A]
              for hh in range(hps)]
        vt2 = [jnp.concatenate([vt_ref[hh, 2 * kp], vt_ref[hh, 2 * kp + 1]], axis=1) for hh in range(hps)]
        return attend(k2, vt2, None, carry)

    init = (jnp.full((1, blk), NEG, F32), jnp.zeros((1, blk), F32), jnp.zeros((DV_A, blk), F32)) * len(chains)
    n_far = jnp.maximum(i - 1, 0)
    carry = lax.fori_loop(0, n_far // 2, far_step, init)

    kb_left = jnp.maximum(i - 2, 0)
    kb_prev = jnp.maximum(i - 1, 0)
    has_left = n_far % 2 == 1
    key = lax.broadcasted_iota(jnp.int32, (2 * blk, blk), 0)
    no_prev = jnp.logical_and(i == 0, key < blk)
    left_bias = jnp.where(has_left, jnp.zeros((blk, blk), F32), NEG)
    biases = [jnp.concatenate([left_bias, jnp.where(no_prev, NEG, bias_ref[hh])], axis=0) for hh in range(hps)]
    k3 = [jnp.concatenate([k_block(hh, kb_left), k_block(hh, kb_prev), k_block(hh, i)], axis=0)
          for hh in range(hps)]
    vt3 = [jnp.concatenate([vt_ref[hh, kb_left], vt_ref[hh, kb_prev], vt_ref[hh, i]], axis=1) for hh in range(hps)]
    carry = attend(k3, vt3, biases, carry)
    lam = _lambda(lam_ref, lam_init)
    for hh in range(hps):
        c1, c2 = carry[6 * hh:6 * hh + 3], carry[6 * hh + 3:6 * hh + 6]
        y = _diff_finish(c1[2] / c1[1], c2[2] / c2[1], lam, g_ref[...], lam_init, 0)
        o_ref[:, hh * DV_A:(hh + 1) * DV_A] = y.T


def _attn_prompt(qt, kb, vt, bias_near, lamv, g_col, ya, t_prompt, n_heads, lam_init):
    blk = ATT_BLK
    nq = t_prompt // blk
    hps = ATT_HEADS_PER_STEP
    assert n_heads % hps == 0
    kernel = functools.partial(_attn_prompt_kernel, lam_init=lam_init)
    return pl.pallas_call(
        kernel,
        grid=(n_heads // hps, nq),
        in_specs=[
            pl.BlockSpec((hps, DV_A, blk), lambda h, i: (h, 0, i)),
            pl.BlockSpec((t_prompt, hps * DV_A), lambda h, i: (0, h)),
            pl.BlockSpec((hps, nq, DV_A, blk), lambda h, i: (h, 0, 0, 0)),
            pl.BlockSpec((hps, 2 * blk, blk), lambda h, i: (h, 0, 0)),
            pl.BlockSpec((4, DH_A), lambda h, i: (0, 0)),
            pl.BlockSpec((DV_A, 1), lambda h, i: (0, 0)),
        ],
        out_specs=pl.BlockSpec((blk, hps * DV_A), lambda h, i: (i, h)),
        out_shape=ya,
        compiler_params=_cparams(("parallel", "arbitrary")),
        name="attn_prompt",
    )(qt, kb, vt, bias_near, lamv, g_col)


def _attn_sample_kernel(pt_ref, q_ref, qg_ref, kn_ref, vn_ref, *rest, n_heads, lam_init, group):
    del pt_ref
    kc_refs, vc_refs = rest[:group], rest[group:2 * group]
    blast_ref, bnew_ref, lam_ref, g_ref, _, o_ref, wq_ref, m_ref, l_ref, acc_ref = rest[2 * group:]
    step = pl.program_id(1)
    is_last = step == pl.num_programs(1) - 1
    dec_seq = q_ref.shape[0]
    rows = 2 * dec_seq
    n_pairs = n_heads // 2
    page = kc_refs[0].shape[2] // n_heads
    width = 2 * page
    lane = lax.broadcasted_iota(jnp.int32, (dec_seq, DV_A), 1)

    @pl.when(step == 0)
    def _():
        ones = _group_ones(LANES, DH_A)
        for h in range(n_heads):
            q = q_ref[:, h * DV_A:(h + 1) * DV_A]
            qn = q * lax.rsqrt(_group_sum(q * q, ones) * (1.0 / DH_A) + RMS_EPS) * qg_ref[...] * Q_SCALE
            wq = jnp.concatenate([jnp.where(lane < DH_A, qn, 0.0), jnp.where(lane >= DH_A, qn, 0.0)], axis=0)
            a = h // n_pairs
            wq_ref[h % n_pairs, a * rows:(a + 1) * rows, :] = wq.astype(BF16)
        m_ref[...] = jnp.full(m_ref.shape, NEG, F32)
        l_ref[...] = jnp.zeros(l_ref.shape, F32)
        acc_ref[...] = jnp.zeros(acc_ref.shape, F32)

    col = lax.broadcasted_iota(jnp.int32, (2 * rows, width), 1)
    row = lax.broadcasted_iota(jnp.int32, (2 * rows, width), 0)
    other_head = jnp.where(col % 2 == row // rows, 0.0, NEG)
    pair_rows = lambda ref, p: ref[0, 0, pl.ds(p, width, stride=n_pairs), :].astype(BF16)

    pairs = range(n_pairs)
    s = []
    for p in pairs:
        wq = wq_ref[p]
        parts = [lax.dot_general(wq, pair_rows(r, p), _NT, preferred_element_type=F32) for r in kc_refs]
        parts = [x + other_head for x in parts[:-1]] + [parts[-1] + jnp.where(is_last, blast_ref[p], other_head)]
        s.append(jnp.concatenate(parts, axis=1) if group > 1 else parts[0])
    upd = [_softmax_update(s[p], m_ref[p][:, 0:1], l_ref[p][:, 0:1], 1) for p in pairs]
    for p in pairs:
        m, l, alpha, prob = upd[p]
        pv = _dot(prob[:, 0:width], pair_rows(vc_refs[0], p))
        for g in range(1, group):
            pv = pv + _dot(prob[:, g * width:(g + 1) * width], pair_rows(vc_refs[g], p))
        m_ref[p] = jnp.broadcast_to(m, m_ref.shape[1:])
        l_ref[p] = jnp.broadcast_to(l, l_ref.shape[1:])
        acc_ref[p] = alpha * acc_ref[p] + pv

    @pl.when(is_last)
    def _():
        lam = _lambda(lam_ref, lam_init)
        pad = jnp.zeros((page - dec_seq, DV_A), F32)
        for h in range(n_heads):
            p, rs = h % n_pairs, slice((h // n_pairs) * rows, (h // n_pairs + 1) * rows)
            sl = slice(h * DV_A, (h + 1) * DV_A)
            k = jnp.concatenate([kn_ref[0][:, sl], pad], axis=0).astype(BF16)
            v = jnp.concatenate([vn_ref[:, sl], pad], axis=0).astype(BF16)
            sc = lax.dot_general(wq_ref[p, rs, :], k, _NT, preferred_element_type=F32) + bnew_ref[h]
            m, l, alpha, prob = _softmax_update(sc, m_ref[p, rs, 0:1], l_ref[p, rs, 0:1], 1)
            o = (alpha * acc_ref[p, rs, :] + _dot(prob, v)) / l
            o_ref[:, sl] = _diff_finish(o[0:dec_seq], o[dec_seq:rows], lam, g_ref[...], lam_init, 1)


def _attn_sample(page_table, p, qg, k_new, cache_k, cache_v, layer, blast, bnew, lamv, g, ya,
                 t_prompt, dec_seq, n_heads, lam_init):
    n_seq, n_pages = page_table.shape
    depth, n_pool, page = cache_k.shape[:3]
    assert page == LANES and n_heads % 2 == 0
    group = max(g for g in (16, 8, 4, 2, 1) if n_pages % g == 0)
    w_a = n_heads * DV_A
    rows = 2 * dec_seq
    row0 = t_prompt // dec_seq
    ck = cache_k.reshape(depth, n_pool, page * n_heads, DV_A)
    cv = cache_v.reshape(depth, n_pool, page * n_heads, DV_A)
    new_rows = lambda col: (lambda b, s, pt: (row0 + b, col))
    const = lambda *shape: pl.BlockSpec(shape, lambda b, s, pt: (0,) * len(shape))
    cache_spec = lambda g: pl.BlockSpec((1, 1, page * n_heads, DV_A),
                                        lambda b, s, pt: (layer, pt[b, s * group + g], 0, 0))
    kernel = functools.partial(_attn_sample_kernel, n_heads=n_heads, lam_init=lam_init, group=group)
    n_in = 5 + 2 * group + 4
    return pl.pallas_call(
        kernel,
        grid_spec=pltpu.PrefetchScalarGridSpec(
            num_scalar_prefetch=1,
            grid=(n_seq, n_pages // group),
            in_specs=[
                pl.BlockSpec((dec_seq, w_a), new_rows(0)),
                const(1, DV_A),
                pl.BlockSpec((1, dec_seq, w_a), lambda b, s, pt: (layer, b, 0)),
                pl.BlockSpec((dec_seq, w_a), new_rows(2)),
            ] + [cache_spec(g) for g in range(group)] * 2 + [
                const(n_heads // 2, 2 * rows, 2 * page),
                const(n_heads, rows, LANES),
                const(4, DH_A),
                const(1, DV_A),
                pl.BlockSpec(memory_space=pl.ANY),
            ],
            out_specs=pl.BlockSpec((dec_seq, w_a), new_rows(0)),
            scratch_shapes=[
                pltpu.VMEM((n_heads // 2, 2 * rows, DV_A), BF16),
                pltpu.VMEM((n_heads // 2, 2 * rows, LANES), F32),
                pltpu.VMEM((n_heads // 2, 2 * rows, LANES), F32),
                pltpu.VMEM((n_heads // 2, 2 * rows, DV_A), F32),
            ],
        ),
        out_shape=jax.ShapeDtypeStruct(ya.shape, ya.dtype),
        input_output_aliases={n_in: 0},
        compiler_params=_cparams(("parallel", "arbitrary")),
        name="attn_sample",
    )(page_table, p, qg, k_new, p, *([ck] * group), *([cv] * group), blast, bnew, lamv, g, ya)


def _rwkv_prep_kernel(x_ref, prev_ref, ov_ref, mu_ref, w0_ref, a0_ref, kk_ref, ka_ref, rk_ref,
                      w2_ref, a2_ref, g2_ref,
                      r_ref, lw_ref, k_ref, v_ref, kkn_ref, a_ref, g_ref, bonus_ref,
                      *, n_prompt_tiles, dec_seq, w_b):
    i = pl.program_id(0)
    x = x_ref[...]
    rows = lax.broadcasted_iota(jnp.int32, x.shape, 0)
    prev = pltpu.roll(x, 1, axis=0)
    first = jnp.where(i == 0, 0.0, prev_ref[SUBLANES - 1:SUBLANES, :])
    prev = jnp.where(rows == 0, first, prev)
    seq_start = jnp.logical_and(i >= n_prompt_tiles, rows % dec_seq == 0)
    prev = jnp.where(seq_start, ov_ref[...], prev)
    xs = x + (prev - x) * mu_ref[...]

    r = xs[:, 0:w_b]
    k = xs[:, w_b:2 * w_b]
    v = xs[:, 2 * w_b:3 * w_b]
    lora = xs[:, 3 * w_b:3 * w_b + LORA_W]
    z = -(w0_ref[...] + _dot(jnp.tanh(lora).astype(BF16), w2_ref[...]))
    softplus = jnp.maximum(z, 0.0) + jnp.log(1.0 + jnp.exp(-jnp.abs(z)))
    lw_ref[...] = -jnp.exp(-softplus - 0.5)
    a = _sigmoid(a0_ref[...] + _dot(lora.astype(BF16), a2_ref[...]))
    g_ref[...] = _dot(_sigmoid(lora).astype(BF16), g2_ref[...])
    ones = _group_ones(LANES, DH_B)
    kk = k * kk_ref[...]
    k2 = k * (1.0 + (a - 1.0) * ka_ref[...])
    rkr = r * k2 * rk_ref[...]
    for c in range(w_b // LANES):
        sl = slice(c * LANES, (c + 1) * LANES)
        kc = kk[:, sl]
        nrm = jnp.sqrt(_group_sum(kc * kc, ones))
        kkn_ref[:, sl] = kc / jnp.maximum(nrm, 1e-12)
        bonus_ref[:, sl] = _group_sum(rkr[:, sl], ones) * v[:, sl]
    r_ref[...] = r
    k_ref[...] = k2
    v_ref[...] = v
    a_ref[...] = a


def _rwkv_prep(p, col_blk, ov, mu, w0, a0, kk, ka, rk, w2p, a2p, g2p, t_prompt, dec_seq):
    n = p.shape[0]
    bm = ATT_BLK
    w_b = w0.shape[1]
    width = mu.shape[1]
    n_pt = t_prompt // bm
    vec = lambda w: pl.BlockSpec((1, w), lambda i: (0, 0))
    mat = pl.BlockSpec((LORA_W, w_b), lambda i: (0, 0))
    out = pl.BlockSpec((bm, w_b), lambda i: (i, 0))
    kernel = functools.partial(_rwkv_prep_kernel, n_prompt_tiles=n_pt, dec_seq=dec_seq, w_b=w_b)
    return pl.pallas_call(
        kernel,
        grid=(n // bm,),
        in_specs=[
            pl.BlockSpec((bm, width), lambda i: (i, col_blk)),
            pl.BlockSpec((SUBLANES, width), lambda i: (jnp.maximum(i * (bm // SUBLANES) - 1, 0), col_blk)),
            pl.BlockSpec((bm, width), lambda i: (jnp.maximum(i - n_pt, 0), 0)),
            vec(width), vec(w_b), vec(w_b), vec(w_b), vec(w_b), vec(w_b), mat, mat, mat,
        ],
        out_specs=[out] * 8,
        out_shape=[jax.ShapeDtypeStruct((n, w_b), F32)] * 8,
        compiler_params=_cparams(("parallel",)),
        name="rwkv_prep",
    )(p, p, ov, mu, w0, a0, kk, ka, rk, w2p, a2p, g2p)


def _rwkv_scan_kernel(r_ref, lw_ref, k_ref, v_ref, kk_ref, a_ref, g_ref, bonus_ref, s0_ref,
                      lng_ref, lnb_ref, yb_in_ref, y_ref, sout_ref, s_ref, *, n_heads, chunk):
    del yb_in_ref
    c_idx = pl.program_id(1)
    n_sub = r_ref.shape[0] // chunk

    @pl.when(c_idx == 0)
    def _():
        s_ref[...] = s0_ref[0]

    ti = lax.broadcasted_iota(jnp.int32, (chunk, chunk), 0)
    si = lax.broadcasted_iota(jnp.int32, (chunk, chunk), 1)
    lower = ti >= si
    strict = ti > si
    eye = (ti == si).astype(F32)
    heads = range(n_heads)
    split = lambda x: [x[:, h * DH_B:(h + 1) * DH_B] for h in heads]
    n_double = max(int(math.log2(chunk)), 1)
    mm = functools.partial(_mm, passes=SCAN_PASSES)

    ah, rh, bh, kh, be, ke, vh, ga = ({} for _ in range(8))
    for j in range(n_sub):
        rows = slice(j * chunk, (j + 1) * chunk)
        lw = lw_ref[rows, :]
        cum = _dot_hi(lower.astype(F32), lw)
        cum_end = cum[chunk - 1:chunk, :]
        g_inv = jnp.exp(-cum)
        g_end = jnp.exp(cum_end - cum)
        kk = kk_ref[rows, :]
        b = kk * a_ref[rows, :]
        k = k_ref[rows, :]
        for name, val in ((ah, -kk * jnp.exp(cum - lw)), (rh, r_ref[rows, :] * jnp.exp(cum)), (bh, b * g_inv),
                          (kh, k * g_inv), (be, b * g_end), (ke, k * g_end), (vh, v_ref[rows, :]),
                          (ga, jnp.exp(cum_end))):
            for h, x in enumerate(split(val)):
                name[j, h] = x
    units = [(j, h) for j in range(n_sub) for h in heads]
    l_ab = {u: jnp.where(strict, mm(ah[u], bh[u], _NT), 0.0) for u in units}
    l_ak = {u: jnp.where(strict, mm(ah[u], kh[u], _NT), 0.0) for u in units}
    m_rb = {u: jnp.where(lower, mm(rh[u], bh[u], _NT), 0.0) for u in units}
    m_rk = {u: jnp.where(lower, mm(rh[u], kh[u], _NT), 0.0) for u in units}
    lp = l_ab
    t_inv = {u: eye + l_ab[u] for u in units}
    for _ in range(n_double - 1):
        lp = {u: mm(lp[u], lp[u]) for u in units}
        t_inv = {u: t_inv[u] + mm(t_inv[u], lp[u]) for u in units}
    w = {u: mm(l_ak[u], vh[u]) for u in units}
    p1 = {u: mm(t_inv[u], ah[u]) for u in units}
    u0 = {u: mm(t_inv[u], w[u]) for u in units}
    y0 = {u: mm(m_rk[u], vh[u]) for u in units}
    h0 = {u: mm(vh[u], ke[u], _TN) for u in units}

    s = [s_ref[h] for h in heads]
    for j in range(n_sub):
        u = [mm(p1[j, h], s[h], _NT) + u0[j, h] for h in heads]
        y = [mm(rh[j, h], s[h], _NT) + mm(m_rb[j, h], u[h]) + y0[j, h] for h in heads]
        s = [s[h] * ga[j, h] + mm(u[h], be[j, h], _TN) + h0[j, h] for h in heads]
        for h in heads:
            yc = y[h] - jnp.mean(y[h], axis=-1, keepdims=True)
            y_ref[j * chunk:(j + 1) * chunk, h * DH_B:(h + 1) * DH_B] = (
                yc * lax.rsqrt(jnp.mean(yc * yc, axis=-1, keepdims=True) + LNX_EPS))
    for h in heads:
        s_ref[h] = s[h]
        sout_ref[0, h] = s[h]

    y_ref[...] = (y_ref[...] * lng_ref[...] + lnb_ref[...] + bonus_ref[...]) * g_ref[...]


def _rwkv_scan(streams, s0, lng, lnb, yb, row0, n_seq, seq_len, chunk, n_sub):
    n, w_b = streams[0].shape
    n_heads = w_b // DH_B
    blk_rows = chunk * n_sub
    n_steps = seq_len // blk_rows
    blk0 = row0 // blk_rows
    rows = pl.BlockSpec((blk_rows, w_b), lambda b, c: (blk0 + b * n_steps + c, 0))
    vec = pl.BlockSpec((1, w_b), lambda b, c: (0, 0))
    state = pl.BlockSpec((1, n_heads, DH_B, DH_B), lambda b, c: (b, 0, 0, 0))
    in_specs = [rows] * 8 + [state, vec, vec]
    args = list(streams) + [s0, lng, lnb]
    aliases = {}
    if yb is not None:
        in_specs.append(pl.BlockSpec(memory_space=pl.ANY))
        args.append(yb)
        aliases = {11: 0}
    kernel = functools.partial(_rwkv_scan_kernel, n_heads=n_heads, chunk=chunk)
    if yb is None:
        kernel = functools.partial(_scan_no_alias, kernel)
    return pl.pallas_call(
        kernel,
        grid=(n_seq, n_steps),
        in_specs=in_specs,
        out_specs=[rows, state],
        out_shape=[jax.ShapeDtypeStruct((n, w_b), F32),
                   jax.ShapeDtypeStruct((n_seq, n_heads, DH_B, DH_B), F32)],
        scratch_shapes=[pltpu.VMEM((n_heads, DH_B, DH_B), F32)],
        input_output_aliases=aliases,
        compiler_params=_cparams(("parallel", "arbitrary")),
        name="rwkv_scan",
    )(*args)


def _scan_no_alias(kernel, *refs):
    return kernel(*refs[:11], None, *refs[11:])


def _glu(pc, w_c):
    return pc[:, 0:w_c] * _sigmoid(pc[:, w_c:2 * w_c])


def _conv_norm_act(c, b_ref, lg_ref, lb_ref):
    c = c + b_ref[...]
    mu = jnp.mean(c, axis=-1, keepdims=True)
    cc = c - mu
    var = jnp.mean(cc * cc, axis=-1, keepdims=True)
    c = cc * lax.rsqrt(var + LN_EPS) * lg_ref[...] + lb_ref[...]
    return c * _sigmoid(c)


def _conv_prompt_kernel(pc_ref, halo_ref, w_ref, b_ref, lg_ref, lb_ref, y_ref, st_ref, buf_ref, sh_ref, *, sub):
    i = pl.program_id(0)
    bm, w_c = y_ref.shape
    halo = halo_ref.shape[0]
    n_buf = halo + bm
    buf_ref[0:halo, :] = jnp.where(i == 0, 0.0, _glu(halo_ref[...], w_c))
    buf_ref[halo:n_buf, :] = _glu(pc_ref[...], w_c)
    buf_ref[n_buf:n_buf + SUBLANES, :] = jnp.zeros((SUBLANES, w_c), F32)
    for r in range(1, SUBLANES):
        sh_ref[r - 1] = buf_ref[r:r + n_buf, :]
    off = halo - (CONV_K - 1)
    for s in range(bm // sub):
        acc = jnp.zeros((sub, w_c), F32)
        for j in range(CONV_K):
            r = (off + j) % SUBLANES
            q = s * sub + off + j - r
            src = buf_ref[q:q + sub, :] if r == 0 else sh_ref[r - 1, q:q + sub, :]
            acc = acc + w_ref[j:j + 1, :] * src
        y_ref[s * sub:(s + 1) * sub, :] = _conv_norm_act(acc, b_ref, lg_ref, lb_ref)
    st_ref[0] = buf_ref[n_buf - (CONV_K - 1):n_buf, :]


def _conv_prompt(p, col_blk, w, b, lg, lb, n_rows, t_prompt):
    bm = ATT_BLK
    halo = 32
    w_c = w.shape[1]
    vec = pl.BlockSpec((1, w_c), lambda i: (0, 0))
    return pl.pallas_call(
        functools.partial(_conv_prompt_kernel, sub=64),
        grid=(t_prompt // bm,),
        in_specs=[
            pl.BlockSpec((bm, 2 * w_c), lambda i: (i, col_blk)),
            pl.BlockSpec((halo, 2 * w_c), lambda i: (jnp.maximum(i * (bm // halo) - 1, 0), col_blk)),
            pl.BlockSpec((CONV_K, w_c), lambda i: (0, 0)),
            vec, vec, vec,
        ],
        out_specs=[
            pl.BlockSpec((bm, w_c), lambda i: (i, 0)),
            pl.BlockSpec((1, CONV_K - 1, w_c), lambda i: (0, 0, 0)),
        ],
        out_shape=[
            jax.ShapeDtypeStruct((n_rows, w_c), F32),
            jax.ShapeDtypeStruct((1, CONV_K - 1, w_c), F32),
        ],
        scratch_shapes=[pltpu.VMEM((halo + bm + SUBLANES, w_c), F32),
                        pltpu.VMEM((SUBLANES - 1, halo + bm, w_c), F32)],
        compiler_params=_cparams(("arbitrary",)),
        name="conv_prompt",
    )(p, p, w, b, lg, lb)


def _conv_sample_kernel(pc_ref, st0_ref, w_ref, b_ref, lg_ref, lb_ref, yc_in_ref, y_ref, st_ref, buf_ref):
    del yc_in_ref
    dec_seq, w_c = y_ref.shape
    hist = CONV_K - 1
    buf_ref[0:hist, :] = st0_ref[0]
    buf_ref[hist:hist + dec_seq, :] = _glu(pc_ref[...], w_c)
    acc = jnp.zeros((dec_seq, w_c), F32)
    for j in range(CONV_K):
        acc = acc + w_ref[j:j + 1, :] * buf_ref[j:j + dec_seq, :]
    y_ref[...] = _conv_norm_act(acc, b_ref, lg_ref, lb_ref)
    st_ref[0] = buf_ref[dec_seq:dec_seq + hist, :]


def _conv_sample(p, col_blk, st0, w, b, lg, lb, yc, t_prompt, dec_seq):
    n_seq = st0.shape[0]
    w_c = w.shape[1]
    hist = CONV_K - 1
    row0 = t_prompt // dec_seq
    vec = pl.BlockSpec((1, w_c), lambda s: (0, 0))
    return pl.pallas_call(
        _conv_sample_kernel,
        grid=(n_seq,),
        in_specs=[
            pl.BlockSpec((dec_seq, 2 * w_c), lambda s: (row0 + s, col_blk)),
            pl.BlockSpec((1, hist, w_c), lambda s: (s, 0, 0)),
            pl.BlockSpec((CONV_K, w_c), lambda s: (0, 0)),
            vec, vec, vec,
            pl.BlockSpec(memory_space=pl.ANY),
        ],
        out_specs=[
            pl.BlockSpec((dec_seq, w_c), lambda s: (row0 + s, 0)),
            pl.BlockSpec((1, hist, w_c), lambda s: (s, 0, 0)),
        ],
        out_shape=[
            jax.ShapeDtypeStruct(yc.shape, yc.dtype),
            jax.ShapeDtypeStruct((n_seq, hist, w_c), F32),
        ],
        scratch_shapes=[pltpu.VMEM((hist + dec_seq + 2, w_c), F32)],
        input_output_aliases={6: 0},
        compiler_params=_cparams(("parallel",)),
        name="conv_sample",
    )(p, st0, w, b, lg, lb, yc)


def _outproj_kernel(x_ref, ya_ref, yb_ref, yc_ref, wa_ref, wb_ref, wc_ref, o_ref):
    acc = x_ref[...] + _dot(ya_ref[...].astype(BF16), wa_ref[...])
    acc = acc + _dot(yb_ref[...].astype(BF16), wb_ref[...])
    o_ref[...] = acc + _dot(yc_ref[...].astype(BF16), wc_ref[...])


def _outproj(x, ya, yb, yc, wo, bm, bn):
    n, d = x.shape
    wa, wb, wc = ya.shape[1], yb.shape[1], yc.shape[1]
    row = lambda w: pl.BlockSpec((bm, w), lambda i, j: (i, 0))
    return pl.pallas_call(
        _outproj_kernel,
        grid=(n // bm, d // bn),
        in_specs=[
            pl.BlockSpec((bm, bn), lambda i, j: (i, j)),
            row(wa), row(wb), row(wc),
            pl.BlockSpec((wa, bn), lambda i, j: (0, j)),
            pl.BlockSpec((wb, bn), lambda i, j: (wa // wb, j)),
            pl.BlockSpec((wc, bn), lambda i, j: ((wa + wb) // wc, j)),
        ],
        out_specs=pl.BlockSpec((bm, bn), lambda i, j: (i, j)),
        out_shape=jax.ShapeDtypeStruct((n, d), F32),
        compiler_params=_cparams(("parallel", "arbitrary")),
        name="outproj",
    )(x, ya, yb, yc, wo, wo, wo)


def _ffn_kernel(x_ref, g_ref, wu_ref, wd_ref, o_ref, h_ref):
    f = pl.program_id(1)

    @pl.when(f == 0)
    def _():
        x = x_ref[...]
        ms = jnp.mean(x * x, axis=-1, keepdims=True)
        h_ref[...] = (x * lax.rsqrt(ms + RMS_EPS) * g_ref[...]).astype(BF16)
        o_ref[...] = x

    u = jnp.maximum(_dot(h_ref[...], wu_ref[...]), 0.0)
    o_ref[...] += _dot((u * u).astype(BF16), wd_ref[...])


def _ffn(x, g, wu, wd, bm, bf):
    n, d = x.shape
    d_ff = wu.shape[1]
    return pl.pallas_call(
        _ffn_kernel,
        grid=(n // bm, d_ff // bf),
        in_specs=[
            pl.BlockSpec((bm, d), lambda i, f: (i, 0)),
            pl.BlockSpec((1, d), lambda i, f: (0, 0)),
            pl.BlockSpec((d, bf), lambda i, f: (0, f)),
            pl.BlockSpec((bf, d), lambda i, f: (f, 0)),
        ],
        out_specs=pl.BlockSpec((bm, d), lambda i, f: (i, 0)),
        out_shape=jax.ShapeDtypeStruct((n, d), F32),
        scratch_shapes=[pltpu.VMEM((bm, d), BF16)],
        compiler_params=_cparams(("parallel", "arbitrary")),
        name="ffn",
    )(x, g, wu, wd)


def _row_tile(n, target):
    best = SUBLANES
    for t in range(SUBLANES, target + 1, SUBLANES):
        if n % t == 0:
            best = t
    return best


def _lambda_init(l):
    return 0.8 - 0.6 * math.exp(-0.3 * l)


def _pad_rows(w, start, total):
    return jnp.zeros((total, w.shape[1]), w.dtype).at[start:start + w.shape[0]].set(w)


def kernel(x_prompt, x_sample, cache_k, cache_v, state_rwkv, state_shift, state_conv, page_table, rel_bias, norm_mix_g, w_in, q_norm_g, k_norm_g, lambda_q1, lambda_k1, lambda_q2, lambda_k2, subln_g, rwkv_mu, rwkv_w0, rwkv_w2, rwkv_a0, rwkv_a2, rwkv_g2, rwkv_k_k, rwkv_k_a, rwkv_r_k, rwkv_lnx_g, rwkv_lnx_b, conv_dw_w, conv_dw_b, conv_ln_g, conv_ln_b, w_out, norm_ffn_g, w_up, w_down):
    depth = w_in.shape[0]
    b_p, t_p, d = x_prompt.shape
    n_seq, dec_seq, _ = x_sample.shape
    assert b_p == 1 and dec_seq == SUBLANES and t_p % ATT_BLK == 0 and (n_seq * dec_seq) % ATT_BLK == 0
    w_a, w_b, w_c = d // 2, d // 4, d // 4
    n_heads = w_a // DV_A
    h_b = w_b // DH_B
    shift_w = state_shift.shape[-1]
    assert shift_w == 3 * w_b + DECAY_LORA + AAA_LORA + GATE_LORA and shift_w <= 3 * w_b + LORA_W
    shift_pad = 3 * w_b + LORA_W
    page = cache_k.shape[2]
    assert page == LANES and page >= MAX_DISTANCE and ATT_BLK >= MAX_DISTANCE
    n = t_p + n_seq * dec_seq
    bm = _row_tile(n, 1024)

    x = jnp.concatenate([x_prompt.reshape(t_p, d), x_sample.reshape(n_seq * dec_seq, d)], axis=0)
    bias_near, bias_last, bias_new = _bias_tiles(rel_bias, n_heads, ATT_BLK, page, dec_seq)

    a_end, b_end = 3 * w_a, 3 * w_a + shift_w
    col_c = 3 * w_a // (2 * w_c)
    col_b = (3 * w_a + 2 * w_c) // shift_pad
    assert col_c * 2 * w_c == 3 * w_a and col_b * shift_pad == 3 * w_a + 2 * w_c

    sp_l, ss_l, shp_l, shs_l, cvp_l, cvs_l = ([] for _ in range(6))
    kv_out = None
    for l in range(depth):
        w_in_p = _win_prep(w_in, l, a_end, b_end, shift_pad - shift_w)
        p = _inproj(x, norm_mix_g[l][None], w_in_p, bm, 2048)

        lam_init = _lambda_init(l)
        qg = jnp.tile(q_norm_g[l], 2)[None]
        kg = jnp.tile(k_norm_g[l], 2)[None]
        lamv = jnp.stack([lambda_q1[l], lambda_k1[l], lambda_q2[l], lambda_k2[l]])
        sg = subln_g[l][None]
        qt, kb, vt, *kv_out = _qkv_prep(p, qg, kg, n_heads, l, depth, t_p, kv_out)
        ya = _attn_prompt(qt, kb, vt, bias_near, lamv, subln_g[l][:, None], jax.ShapeDtypeStruct((n, w_a), F32),
                          t_p, n_heads, lam_init)
        ya = _attn_sample(page_table, p, qg, kv_out[2], cache_k, cache_v, l, bias_last, bias_new, lamv, sg, ya,
                          t_p, dec_seq, n_heads, lam_init)

        pad = ((0, 0), (0, shift_pad - shift_w))
        ov = jnp.pad(jnp.repeat(state_shift[l], dec_seq, axis=0), pad)
        lo = 0
        w2p = _pad_rows(rwkv_w2[l], lo, LORA_W).astype(BF16)
        a2p = _pad_rows(rwkv_a2[l], lo + DECAY_LORA, LORA_W).astype(BF16)
        g2p = _pad_rows(rwkv_g2[l], lo + DECAY_LORA + AAA_LORA, LORA_W).astype(BF16)
        streams = _rwkv_prep(p, col_b, ov, jnp.pad(rwkv_mu[l][None], pad), rwkv_w0[l][None], rwkv_a0[l][None],
                             rwkv_k_k[l][None], rwkv_k_a[l][None], rwkv_r_k[l].reshape(1, w_b),
                             w2p, a2p, g2p, t_p, dec_seq)
        lng, lnb = rwkv_lnx_g[l][None], rwkv_lnx_b[l][None]
        yb, s_p = _rwkv_scan(streams, jnp.zeros((b_p, h_b, DH_B, DH_B), F32), lng, lnb, None,
                             0, b_p, t_p, RW_CHUNK, RW_CHUNKS_PER_STEP)
        yb, s_s = _rwkv_scan(streams, state_rwkv[l], lng, lnb, yb, t_p, n_seq, dec_seq, dec_seq, 1)
        sp_l.append(s_p)
        ss_l.append(s_s)
        c0 = col_b * shift_pad
        shp_l.append(p[t_p - 1:t_p, c0:c0 + shift_w])
        shs_l.append(p[t_p + dec_seq - 1::dec_seq, c0:c0 + shift_w])

        cw, cb = conv_dw_w[l], conv_dw_b[l][None]
        clg, clb = conv_ln_g[l][None], conv_ln_b[l][None]
        yc, cv_p = _conv_prompt(p, col_c, cw, cb, clg, clb, n, t_p)
        yc, cv_s = _conv_sample(p, col_c, state_conv[l], cw, cb, clg, clb, yc, t_p, dec_seq)
        cvp_l.append(cv_p)
        cvs_l.append(cv_s)

        x = _outproj(x, ya, yb, yc, _cast_bf16(w_out, l), _row_tile(n, 512), d)
        x = _ffn(x, norm_ffn_g[l][None], _cast_bf16(w_up, l), _cast_bf16(w_down, l), bm, 1024)

    y_prompt = x[:t_p].reshape(b_p, t_p, d)
    y_sample = x[t_p:].reshape(n_seq, dec_seq, d)
    k_p, v_p, k_s, v_s = kv_out
    return (y_prompt, y_sample,
            k_p.reshape(depth, b_p, t_p, n_heads, DV_A), v_p.reshape(depth, b_p, t_p, n_heads, DV_A),
            k_s.reshape(depth, n_seq, dec_seq, n_heads, DV_A), v_s.reshape(depth, n_seq, dec_seq, n_heads, DV_A),
            jnp.stack(sp_l), jnp.stack(ss_l), jnp.stack(shp_l), jnp.stack(shs_l),
            jnp.stack(cvp_l), jnp.stack(cvs_l))
```

```python
import functools
import math

import jax
import jax.numpy as jnp
from jax import lax
from jax.experimental import pallas as pl
from jax.experimental.pallas import tpu as pltpu

F32 = jnp.float32
BF16 = jnp.bfloat16

DH_A = 64
DV_A = 2 * DH_A
DH_B = 64
DECAY_LORA = 96
AAA_LORA = 96
GATE_LORA = 256
LORA_W = 512
CONV_K = 31
NUM_BUCKETS = 32
MAX_DISTANCE = 128
RMS_EPS = 1e-6
LN_EPS = 1e-5
LNX_EPS = 64e-5
NEG = -1e30
LOG2E = math.log2(math.e)
Q_SCALE = DH_A ** -0.5 * LOG2E

LANES = 128
SUBLANES = 8
VMEM_LIMIT = 56 * 1024 * 1024

ATT_BLK = 256
RW_CHUNK = 64
RW_CHUNKS_PER_STEP = 4
SCAN_PASSES = 1
ATT_HEADS_PER_STEP = 4


def _cparams(sem):
    return pltpu.CompilerParams(dimension_semantics=sem, vmem_limit_bytes=VMEM_LIMIT)


def _dot(a, b):
    return jnp.dot(a, b, preferred_element_type=F32)


def _dot_hi(a, b, dims=(((1,), (0,)), ((), ()))):
    return lax.dot_general(a, b, dims, preferred_element_type=F32, precision=lax.Precision.HIGHEST)


_NT = (((1,), (1,)), ((), ()))
_TN = (((0,), (0,)), ((), ()))
_NN = (((1,), (0,)), ((), ()))


def _mm(a, b, dims=_NN, passes=1):
    dot = lambda x, y: lax.dot_general(x, y, dims, preferred_element_type=F32)
    ah, bh = a.astype(BF16), b.astype(BF16)
    out = dot(ah, bh)
    if passes == 3:
        al = (a - ah.astype(F32)).astype(BF16)
        bl = (b - bh.astype(F32)).astype(BF16)
        out = out + dot(ah, bl) + dot(al, bh)
    return out


def _group_ones(width, group):
    r = lax.broadcasted_iota(jnp.int32, (width, width), 0) // group
    c = lax.broadcasted_iota(jnp.int32, (width, width), 1) // group
    return (r == c).astype(BF16)


def _group_sum(x, ones):
    hi = x.astype(BF16)
    lo = (x - hi.astype(F32)).astype(BF16)
    return _dot(hi, ones) + _dot(lo, ones)


def _sigmoid(x):
    return 1.0 / (1.0 + jnp.exp(-x))


def _cast_kernel(w_ref, o_ref):
    o_ref[...] = w_ref[0].astype(BF16)


def _cast_bf16(w, layer):
    _, rows, cols = w.shape
    bm = _row_tile(rows, max(SUBLANES, (1 << 21) // cols))
    return pl.pallas_call(
        _cast_kernel,
        grid=(rows // bm,),
        in_specs=[pl.BlockSpec((1, bm, cols), lambda i: (layer, i, 0))],
        out_specs=pl.BlockSpec((bm, cols), lambda i: (i, 0)),
        out_shape=jax.ShapeDtypeStruct((rows, cols), BF16),
        compiler_params=_cparams(("parallel",)),
        name="cast_bf16",
    )(w)


def _win_prep_kernel(w_ref, o_ref, *, a_end, b_end, pad):
    n_in = w_ref.shape[2]
    c_w = n_in - b_end
    o_ref[:, 0:a_end] = w_ref[0, :, 0:a_end].astype(BF16)
    o_ref[:, a_end:a_end + c_w] = w_ref[0, :, b_end:n_in].astype(BF16)
    o_ref[:, a_end + c_w:n_in] = w_ref[0, :, a_end:b_end].astype(BF16)
    o_ref[:, n_in:n_in + pad] = jnp.zeros((o_ref.shape[0], pad), BF16)


def _win_prep(w_in, layer, a_end, b_end, pad):
    _, rows, n_in = w_in.shape
    bm = _row_tile(rows, 256)
    return pl.pallas_call(
        functools.partial(_win_prep_kernel, a_end=a_end, b_end=b_end, pad=pad),
        grid=(rows // bm,),
        in_specs=[pl.BlockSpec((1, bm, n_in), lambda i: (layer, i, 0))],
        out_specs=pl.BlockSpec((bm, n_in + pad), lambda i: (i, 0)),
        out_shape=jax.ShapeDtypeStruct((rows, n_in + pad), BF16),
        compiler_params=_cparams(("parallel",)),
        name="win_prep",
    )(w_in)


def _inproj_kernel(x_ref, g_ref, w_ref, o_ref, h_ref):
    @pl.when(pl.program_id(1) == 0)
    def _():
        x = x_ref[...]
        ms = jnp.mean(x * x, axis=-1, keepdims=True)
        h_ref[...] = (x * lax.rsqrt(ms + RMS_EPS) * g_ref[...]).astype(BF16)

    o_ref[...] = _dot(h_ref[...], w_ref[...])


def _inproj(x, g, w, bm, bn):
    n, d = x.shape
    wn = w.shape[1]
    return pl.pallas_call(
        _inproj_kernel,
        grid=(n // bm, wn // bn),
        in_specs=[
            pl.BlockSpec((bm, d), lambda i, j: (i, 0)),
            pl.BlockSpec((1, d), lambda i, j: (0, 0)),
            pl.BlockSpec((d, bn), lambda i, j: (0, j)),
        ],
        out_specs=pl.BlockSpec((bm, bn), lambda i, j: (i, j)),
        out_shape=jax.ShapeDtypeStruct((n, wn), F32),
        scratch_shapes=[pltpu.VMEM((bm, d), BF16)],
        compiler_params=_cparams(("parallel", "arbitrary")),
        name="inproj",
    )(x, g, w)


def _qkv_kernel(p_ref, qg_ref, kg_ref, *rest, n_heads):
    qt_ref, kb_ref, vt_ref, kp_ref, vp_ref, ks_ref, vs_ref = rest[-7:]
    is_sample = pl.program_id(0) == 0
    ones = _group_ones(LANES, DH_A)
    w_a = n_heads * DV_A
    for h in range(n_heads):
        sl = slice(h * DV_A, (h + 1) * DV_A)
        q = p_ref[:, h * DV_A:(h + 1) * DV_A]
        qn = q * lax.rsqrt(_group_sum(q * q, ones) * (1.0 / DH_A) + RMS_EPS) * qg_ref[...]
        qt_ref[h] = (qn * Q_SCALE).T.astype(BF16)
        k = p_ref[:, w_a + h * DV_A:w_a + (h + 1) * DV_A]
        kn = k * lax.rsqrt(_group_sum(k * k, ones) * (1.0 / DH_A) + RMS_EPS) * kg_ref[...]
        kb_ref[:, sl] = kn.astype(BF16)
        v = p_ref[:, 2 * w_a + h * DV_A:2 * w_a + (h + 1) * DV_A]
        vt_ref[h, 0] = v.T.astype(BF16)

        kp_ref[0, :, sl] = kn
        vp_ref[0, :, sl] = v

    @pl.when(is_sample)
    def _():
        ks_ref[...] = kp_ref[...]
        vs_ref[...] = vp_ref[...]


def _qkv_prep(p, qg, kg, n_heads, layer, depth, t_prompt, kv_out):
    n = p.shape[0]
    bm = ATT_BLK
    w_a = n_heads * DV_A
    nb = n // bm
    nb_p = t_prompt // bm
    n_s = n - t_prompt
    assert n_s == bm
    blk = lambda i: jnp.where(i == 0, nb_p, i - 1)
    prompt_blk = lambda i: (layer, jnp.maximum(i - 1, 0), 0)
    sample_blk = lambda i: (layer, 0, 0)
    in_specs = [
        pl.BlockSpec((bm, 3 * w_a), lambda i: (blk(i), 0)),
        pl.BlockSpec((1, DV_A), lambda i: (0, 0)),
        pl.BlockSpec((1, DV_A), lambda i: (0, 0)),
    ]
    args = [p, qg, kg]
    aliases = {}
    if kv_out is not None:
        in_specs += [pl.BlockSpec(memory_space=pl.ANY)] * 4
        args += list(kv_out)
        aliases = {3 + j: 3 + j for j in range(4)}
    return pl.pallas_call(
        functools.partial(_qkv_kernel, n_heads=n_heads),
        grid=(nb,),
        in_specs=in_specs,
        out_specs=[
            pl.BlockSpec((n_heads, DV_A, bm), lambda i: (0, 0, blk(i))),
            pl.BlockSpec((bm, w_a), lambda i: (blk(i), 0)),
            pl.BlockSpec((n_heads, 1, DV_A, bm), lambda i: (0, blk(i), 0, 0)),
            pl.BlockSpec((1, bm, w_a), prompt_blk),
            pl.BlockSpec((1, bm, w_a), prompt_blk),
            pl.BlockSpec((1, bm, w_a), sample_blk),
            pl.BlockSpec((1, bm, w_a), sample_blk),
        ],
        out_shape=[
            jax.ShapeDtypeStruct((n_heads, DV_A, n), BF16),
            jax.ShapeDtypeStruct((n, w_a), BF16),
            jax.ShapeDtypeStruct((n_heads, nb, DV_A, bm), BF16),
            jax.ShapeDtypeStruct((depth, t_prompt, w_a), F32),
            jax.ShapeDtypeStruct((depth, t_prompt, w_a), F32),
            jax.ShapeDtypeStruct((depth, n_s, w_a), F32),
            jax.ShapeDtypeStruct((depth, n_s, w_a), F32),
        ],
        input_output_aliases=aliases,
        compiler_params=_cparams(("arbitrary",)),
        name="qkv_prep",
    )(*args)


def _t5_bucket(n):
    max_exact = NUM_BUCKETS // 2
    nf = jnp.maximum(n, 1).astype(F32)
    large = max_exact + (jnp.log(nf / max_exact) / math.log(MAX_DISTANCE / max_exact)
                         * (NUM_BUCKETS - max_exact)).astype(jnp.int32)
    large = jnp.minimum(large, NUM_BUCKETS - 1)
    return jnp.where(n < max_exact, n, large)


def _bias_from_distance(n, rb_ref, h, n_heads):
    bucket = _t5_bucket(jnp.maximum(n, 0))
    far = rb_ref[(NUM_BUCKETS - 1) * n_heads + h]
    val = jnp.zeros(n.shape, F32)
    for b in range(NUM_BUCKETS - 1):
        val = jnp.where(bucket == b, (rb_ref[b * n_heads + h] - far) * LOG2E, val)
    return jnp.where(n < 0, NEG, val)


def _bias_kernel(rb_ref, near_ref, last_ref, new_ref, *, n_heads, blk, page, dec_seq):
    r = lax.broadcasted_iota(jnp.int32, (2 * blk, blk), 0)
    c = lax.broadcasted_iota(jnp.int32, (2 * blk, blk), 1)
    n_near = c - r + blk
    for h in range(n_heads):
        near_ref[h] = _bias_from_distance(n_near, rb_ref, h, n_heads)
    rows = 2 * dec_seq
    n_pairs = n_heads // 2
    t = lax.broadcasted_iota(jnp.int32, (rows, 2 * page), 0) % dec_seq
    col = lax.broadcasted_iota(jnp.int32, (rows, 2 * page), 1)
    n_last = page + t - col // 2
    for p in range(n_pairs):
        for a in range(2):
            tile = _bias_from_distance(n_last, rb_ref, p + a * n_pairs, n_heads)
            last_ref[p, a * rows:(a + 1) * rows, :] = jnp.where(col % 2 == a, tile, NEG)
    t = lax.broadcasted_iota(jnp.int32, (rows, LANES), 0) % dec_seq
    j = lax.broadcasted_iota(jnp.int32, (rows, LANES), 1)
    n_new = jnp.where(j < dec_seq, t - j, -1)
    for h in range(n_heads):
        new_ref[h] = _bias_from_distance(n_new, rb_ref, h, n_heads)


def _bias_tiles(rel_bias, n_heads, blk, page, dec_seq):
    rows = 2 * dec_seq
    return pl.pallas_call(
        functools.partial(_bias_kernel, n_heads=n_heads, blk=blk, page=page, dec_seq=dec_seq),
        in_specs=[pl.BlockSpec(memory_space=pltpu.SMEM)],
        out_shape=[
            jax.ShapeDtypeStruct((n_heads, 2 * blk, blk), F32),
            jax.ShapeDtypeStruct((n_heads // 2, 2 * rows, 2 * page), F32),
            jax.ShapeDtypeStruct((n_heads, rows, LANES), F32),
        ],
        compiler_params=pltpu.CompilerParams(vmem_limit_bytes=VMEM_LIMIT),
        name="bias_tiles",
    )(rel_bias.reshape(-1))


def _lambda(lam_ref, lam_init):
    l = lam_ref[...]
    s1 = jnp.sum(l[0:1] * l[1:2], axis=-1, keepdims=True)
    s2 = jnp.sum(l[2:3] * l[3:4], axis=-1, keepdims=True)
    return jnp.exp(s1) - jnp.exp(s2) + lam_init


def _softmax_update(s, m, l, axis):
    m_new = jnp.maximum(m, jnp.max(s, axis=axis, keepdims=True))
    p = jnp.exp2(s - m_new)
    alpha = jnp.exp2(m - m_new)
    return m_new, alpha * l + jnp.sum(p, axis=axis, keepdims=True), alpha, p.astype(BF16)


def _diff_finish(o1, o2, lam, g, lam_init, axis):
    o = o1 - lam * o2
    ms = jnp.mean(o * o, axis=axis, keepdims=True)
    return o * lax.rsqrt(ms + RMS_EPS) * g * (1.0 - lam_init)


def _attn_prompt_kernel(qt_ref, k_ref, vt_ref, bias_ref, lam_ref, g_ref, o_ref, *, lam_init):
    i = pl.program_id(1)
    blk = o_ref.shape[0]
    hps = qt_ref.shape[0]
    chains = [(hh, c) for hh in range(hps) for c in range(2)]

    comp = lax.broadcasted_iota(jnp.int32, (DV_A, blk), 0) // DH_A
    zero = jnp.zeros((DV_A, blk), BF16)
    qc = [jnp.where(comp == c, qt_ref[hh], zero) for hh, c in chains]

    def k_block(hh, kb):
        return k_ref[pl.ds(pl.multiple_of(kb * blk, blk), blk), hh * DV_A:(hh + 1) * DV_A]

    def attend(ks, vts, biases, carry):
        s = [_dot(ks[hh], qc[n]) for n, (hh, c) in enumerate(chains)]
        if biases is not None:
            s = [s[n] + biases[hh] for n, (hh, c) in enumerate(chains)]
        upd = [_softmax_update(s[n], carry[3 * n], carry[3 * n + 1], 0) for n in range(len(chains))]
        out = []
        for n, (hh, c) in enumerate(chains):
            m, l, alpha, p = upd[n]
            out.extend((m, l, alpha * carry[3 * n + 2] + _dot(vts[hh], p)))
        return tuple(out)

    def far_step(kp, carry):
        k2 = [k_ref[pl.ds(pl.multiple_of(kp * 2 * blk, 2 * blk), 2 * blk), hh * DV_A:(hh + 1) * DV_A]
              for hh in range(hps)]
        vt2 = [jnp.concatenate([vt_ref[hh, 2 * kp], vt_ref[hh, 2 * kp + 1]], axis=1) for hh in range(hps)]
        return attend(k2, vt2, None, carry)

    init = (jnp.full((1, blk), NEG, F32), jnp.zeros((1, blk), F32), jnp.zeros((DV_A, blk), F32)) * len(chains)
    n_far = jnp.maximum(i - 1, 0)
    carry = lax.fori_loop(0, n_far // 2, far_step, init)

    kb_left = jnp.maximum(i - 2, 0)
    kb_prev = jnp.maximum(i - 1, 0)
    has_left = n_far % 2 == 1
    key = lax.broadcasted_iota(jnp.int32, (2 * blk, blk), 0)
    no_prev = jnp.logical_and(i == 0, key < blk)
    left_bias = jnp.where(has_left, jnp.zeros((blk, blk), F32), NEG)
    biases = [jnp.concatenate([left_bias, jnp.where(no_prev, NEG, bias_ref[hh])], axis=0) for hh in range(hps)]
    k3 = [jnp.concatenate([k_block(hh, kb_left), k_block(hh, kb_prev), k_block(hh, i)], axis=0)
          for hh in range(hps)]
    vt3 = [jnp.concatenate([vt_ref[hh, kb_left], vt_ref[hh, kb_prev], vt_ref[hh, i]], axis=1) for hh in range(hps)]
    carry = attend(k3, vt3, biases, carry)
    lam = _lambda(lam_ref, lam_init)
    for hh in range(hps):
        c1, c2 = carry[6 * hh:6 * hh + 3], carry[6 * hh + 3:6 * hh + 6]
        y = _diff_finish(c1[2] / c1[1], c2[2] / c2[1], lam, g_ref[...], lam_init, 0)
        o_ref[:, hh * DV_A:(hh + 1) * DV_A] = y.T


def _attn_prompt(qt, kb, vt, bias_near, lamv, g_col, ya, t_prompt, n_heads, lam_init):
    blk = ATT_BLK
    nq = t_prompt // blk
    hps = ATT_HEADS_PER_STEP
    assert n_heads % hps == 0
    kernel = functools.partial(_attn_prompt_kernel, lam_init=lam_init)
    return pl.pallas_call(
        kernel,
        grid=(n_heads // hps, nq),
        in_specs=[
            pl.BlockSpec((hps, DV_A, blk), lambda h, i: (h, 0, i)),
            pl.BlockSpec((t_prompt, hps * DV_A), lambda h, i: (0, h)),
            pl.BlockSpec((hps, nq, DV_A, blk), lambda h, i: (h, 0, 0, 0)),
            pl.BlockSpec((hps, 2 * blk, blk), lambda h, i: (h, 0, 0)),
            pl.BlockSpec((4, DH_A), lambda h, i: (0, 0)),
            pl.BlockSpec((DV_A, 1), lambda h, i: (0, 0)),
        ],
        out_specs=pl.BlockSpec((blk, hps * DV_A), lambda h, i: (i, h)),
        out_shape=ya,
        compiler_params=_cparams(("parallel", "arbitrary")),
        name="attn_prompt",
    )(qt, kb, vt, bias_near, lamv, g_col)


def _attn_sample_kernel(pt_ref, q_ref, qg_ref, kn_ref, vn_ref, *rest, n_heads, lam_init, group):
    del pt_ref
    kc_refs, vc_refs = rest[:group], rest[group:2 * group]
    blast_ref, bnew_ref, lam_ref, g_ref, _, o_ref, wq_ref, m_ref, l_ref, acc_ref = rest[2 * group:]
    step = pl.program_id(1)
    is_last = step == pl.num_programs(1) - 1
    dec_seq = q_ref.shape[0]
    rows = 2 * dec_seq
    n_pairs = n_heads // 2
    page = kc_refs[0].shape[2] // n_heads
    width = 2 * page
    lane = lax.broadcasted_iota(jnp.int32, (dec_seq, DV_A), 1)

    @pl.when(step == 0)
    def _():
        ones = _group_ones(LANES, DH_A)
        for h in range(n_heads):
            q = q_ref[:, h * DV_A:(h + 1) * DV_A]
            qn = q * lax.rsqrt(_group_sum(q * q, ones) * (1.0 / DH_A) + RMS_EPS) * qg_ref[...] * Q_SCALE
            wq = jnp.concatenate([jnp.where(lane < DH_A, qn, 0.0), jnp.where(lane >= DH_A, qn, 0.0)], axis=0)
            a = h // n_pairs
            wq_ref[h % n_pairs, a * rows:(a + 1) * rows, :] = wq.astype(BF16)
        m_ref[...] = jnp.full(m_ref.shape, NEG, F32)
        l_ref[...] = jnp.zeros(l_ref.shape, F32)
        acc_ref[...] = jnp.zeros(acc_ref.shape, F32)

    col = lax.broadcasted_iota(jnp.int32, (2 * rows, width), 1)
    row = lax.broadcasted_iota(jnp.int32, (2 * rows, width), 0)
    other_head = jnp.where(col % 2 == row // rows, 0.0, NEG)
    pair_rows = lambda ref, p: ref[0, 0, pl.ds(p, width, stride=n_pairs), :].astype(BF16)

    pairs = range(n_pairs)
    s = []
    for p in pairs:
        wq = wq_ref[p]
        parts = [lax.dot_general(wq, pair_rows(r, p), _NT, preferred_element_type=F32) for r in kc_refs]
        parts = [x + other_head for x in parts[:-1]] + [parts[-1] + jnp.where(is_last, blast_ref[p], other_head)]
        s.append(jnp.concatenate(parts, axis=1) if group > 1 else parts[0])
    upd = [_softmax_update(s[p], m_ref[p][:, 0:1], l_ref[p][:, 0:1], 1) for p in pairs]
    for p in pairs:
        m, l, alpha, prob = upd[p]
        pv = _dot(prob[:, 0:width], pair_rows(vc_refs[0], p))
        for g in range(1, group):
            pv = pv + _dot(prob[:, g * width:(g + 1) * width], pair_rows(vc_refs[g], p))
        m_ref[p] = jnp.broadcast_to(m, m_ref.shape[1:])
        l_ref[p] = jnp.broadcast_to(l, l_ref.shape[1:])
        acc_ref[p] = alpha * acc_ref[p] + pv

    @pl.when(is_last)
    def _():
        lam = _lambda(lam_ref, lam_init)
        pad = jnp.zeros((page - dec_seq, DV_A), F32)
        for h in range(n_heads):
            p, rs = h % n_pairs, slice((h // n_pairs) * rows, (h // n_pairs + 1) * rows)
            sl = slice(h * DV_A, (h + 1) * DV_A)
            k = jnp.concatenate([kn_ref[0][:, sl], pad], axis=0).astype(BF16)
            v = jnp.concatenate([vn_ref[:, sl], pad], axis=0).astype(BF16)
            sc = lax.dot_general(wq_ref[p, rs, :], k, _NT, preferred_element_type=F32) + bnew_ref[h]
            m, l, alpha, prob = _softmax_update(sc, m_ref[p, rs, 0:1], l_ref[p, rs, 0:1], 1)
            o = (alpha * acc_ref[p, rs, :] + _dot(prob, v)) / l
            o_ref[:, sl] = _diff_finish(o[0:dec_seq], o[dec_seq:rows], lam, g_ref[...], lam_init, 1)


def _attn_sample(page_table, p, qg, k_new, cache_k, cache_v, layer, blast, bnew, lamv, g, ya,
                 t_prompt, dec_seq, n_heads, lam_init):
    n_seq, n_pages = page_table.shape
    depth, n_pool, page = cache_k.shape[:3]
    assert page == LANES and n_heads % 2 == 0
    group = max(g for g in (16, 8, 4, 2, 1) if n_pages % g == 0)
    w_a = n_heads * DV_A
    rows = 2 * dec_seq
    row0 = t_prompt // dec_seq
    ck = cache_k.reshape(depth, n_pool, page * n_heads, DV_A)
    cv = cache_v.reshape(depth, n_pool, page * n_heads, DV_A)
    new_rows = lambda col: (lambda b, s, pt: (row0 + b, col))
    const = lambda *shape: pl.BlockSpec(shape, lambda b, s, pt: (0,) * len(shape))
    cache_spec = lambda g: pl.BlockSpec((1, 1, page * n_heads, DV_A),
                                        lambda b, s, pt: (layer, pt[b, s * group + g], 0, 0))
    kernel = functools.partial(_attn_sample_kernel, n_heads=n_heads, lam_init=lam_init, group=group)
    n_in = 5 + 2 * group + 4
    return pl.pallas_call(
        kernel,
        grid_spec=pltpu.PrefetchScalarGridSpec(
            num_scalar_prefetch=1,
            grid=(n_seq, n_pages // group),
            in_specs=[
                pl.BlockSpec((dec_seq, w_a), new_rows(0)),
                const(1, DV_A),
                pl.BlockSpec((1, dec_seq, w_a), lambda b, s, pt: (layer, b, 0)),
                pl.BlockSpec((dec_seq, w_a), new_rows(2)),
            ] + [cache_spec(g) for g in range(group)] * 2 + [
                const(n_heads // 2, 2 * rows, 2 * page),
                const(n_heads, rows, LANES),
                const(4, DH_A),
                const(1, DV_A),
                pl.BlockSpec(memory_space=pl.ANY),
            ],
            out_specs=pl.BlockSpec((dec_seq, w_a), new_rows(0)),
            scratch_shapes=[
                pltpu.VMEM((n_heads // 2, 2 * rows, DV_A), BF16),
                pltpu.VMEM((n_heads // 2, 2 * rows, LANES), F32),
                pltpu.VMEM((n_heads // 2, 2 * rows, LANES), F32),
                pltpu.VMEM((n_heads // 2, 2 * rows, DV_A), F32),
            ],
        ),
        out_shape=jax.ShapeDtypeStruct(ya.shape, ya.dtype),
        input_output_aliases={n_in: 0},
        compiler_params=_cparams(("parallel", "arbitrary")),
        name="attn_sample",
    )(page_table, p, qg, k_new, p, *([ck] * group), *([cv] * group), blast, bnew, lamv, g, ya)


def _rwkv_prep_kernel(x_ref, prev_ref, ov_ref, mu_ref, w0_ref, a0_ref, kk_ref, ka_ref, rk_ref,
                      w2_ref, a2_ref, g2_ref,
                      r_ref, lw_ref, k_ref, v_ref, kkn_ref, a_ref, g_ref, bonus_ref,
                      *, n_prompt_tiles, dec_seq, w_b):
    i = pl.program_id(0)
    x = x_ref[...]
    rows = lax.broadcasted_iota(jnp.int32, x.shape, 0)
    prev = pltpu.roll(x, 1, axis=0)
    first = jnp.where(i == 0, 0.0, prev_ref[SUBLANES - 1:SUBLANES, :])
    prev = jnp.where(rows == 0, first, prev)
    seq_start = jnp.logical_and(i >= n_prompt_tiles, rows % dec_seq == 0)
    prev = jnp.where(seq_start, ov_ref[...], prev)
    xs = x + (prev - x) * mu_ref[...]

    r = xs[:, 0:w_b]
    k = xs[:, w_b:2 * w_b]
    v = xs[:, 2 * w_b:3 * w_b]
    lora = xs[:, 3 * w_b:3 * w_b + LORA_W]
    z = -(w0_ref[...] + _dot(jnp.tanh(lora).astype(BF16), w2_ref[...]))
    softplus = jnp.maximum(z, 0.0) + jnp.log(1.0 + jnp.exp(-jnp.abs(z)))
    lw_ref[...] = -jnp.exp(-softplus - 0.5)
    a = _sigmoid(a0_ref[...] + _dot(lora.astype(BF16), a2_ref[...]))
    g_ref[...] = _dot(_sigmoid(lora).astype(BF16), g2_ref[...])
    ones = _group_ones(LANES, DH_B)
    kk = k * kk_ref[...]
    k2 = k * (1.0 + (a - 1.0) * ka_ref[...])
    rkr = r * k2 * rk_ref[...]
    for c in range(w_b // LANES):
        sl = slice(c * LANES, (c + 1) * LANES)
        kc = kk[:, sl]
        nrm = jnp.sqrt(_group_sum(kc * kc, ones))
        kkn_ref[:, sl] = kc / jnp.maximum(nrm, 1e-12)
        bonus_ref[:, sl] = _group_sum(rkr[:, sl], ones) * v[:, sl]
    r_ref[...] = r
    k_ref[...] = k2
    v_ref[...] = v
    a_ref[...] = a


def _rwkv_prep(p, col_blk, ov, mu, w0, a0, kk, ka, rk, w2p, a2p, g2p, t_prompt, dec_seq):
    n = p.shape[0]
    bm = ATT_BLK
    w_b = w0.shape[1]
    width = mu.shape[1]
    n_pt = t_prompt // bm
    vec = lambda w: pl.BlockSpec((1, w), lambda i: (0, 0))
    mat = pl.BlockSpec((LORA_W, w_b), lambda i: (0, 0))
    out = pl.BlockSpec((bm, w_b), lambda i: (i, 0))
    kernel = functools.partial(_rwkv_prep_kernel, n_prompt_tiles=n_pt, dec_seq=dec_seq, w_b=w_b)
    return pl.pallas_call(
        kernel,
        grid=(n // bm,),
        in_specs=[
            pl.BlockSpec((bm, width), lambda i: (i, col_blk)),
            pl.BlockSpec((SUBLANES, width), lambda i: (jnp.maximum(i * (bm // SUBLANES) - 1, 0), col_blk)),
            pl.BlockSpec((bm, width), lambda i: (jnp.maximum(i - n_pt, 0), 0)),
            vec(width), vec(w_b), vec(w_b), vec(w_b), vec(w_b), vec(w_b), mat, mat, mat,
        ],
        out_specs=[out] * 8,
        out_shape=[jax.ShapeDtypeStruct((n, w_b), F32)] * 8,
        compiler_params=_cparams(("parallel",)),
        name="rwkv_prep",
    )(p, p, ov, mu, w0, a0, kk, ka, rk, w2p, a2p, g2p)


def _rwkv_scan_kernel(r_ref, lw_ref, k_ref, v_ref, kk_ref, a_ref, g_ref, bonus_ref, s0_ref,
                      lng_ref, lnb_ref, yb_in_ref, y_ref, sout_ref, s_ref, *, n_heads, chunk):
    del yb_in_ref
    c_idx = pl.program_id(1)
    n_sub = r_ref.shape[0] // chunk

    @pl.when(c_idx == 0)
    def _():
        s_ref[...] = s0_ref[0]

    ti = lax.broadcasted_iota(jnp.int32, (chunk, chunk), 0)
    si = lax.broadcasted_iota(jnp.int32, (chunk, chunk), 1)
    lower = ti >= si
    strict = ti > si
    eye = (ti == si).astype(F32)
    heads = range(n_heads)
    split = lambda x: [x[:, h * DH_B:(h + 1) * DH_B] for h in heads]
    n_double = max(int(math.log2(chunk)), 1)
    mm = functools.partial(_mm, passes=SCAN_PASSES)

    ah, rh, bh, kh, be, ke, vh, ga = ({} for _ in range(8))
    for j in range(n_sub):
        rows = slice(j * chunk, (j + 1) * chunk)
        lw = lw_ref[rows, :]
        cum = _dot_hi(lower.astype(F32), lw)
        cum_end = cum[chunk - 1:chunk, :]
        g_inv = jnp.exp(-cum)
        g_end = jnp.exp(cum_end - cum)
        kk = kk_ref[rows, :]
        b = kk * a_ref[rows, :]
        k = k_ref[rows, :]
        for name, val in ((ah, -kk * jnp.exp(cum - lw)), (rh, r_ref[rows, :] * jnp.exp(cum)), (bh, b * g_inv),
                          (kh, k * g_inv), (be, b * g_end), (ke, k * g_end), (vh, v_ref[rows, :]),
                          (ga, jnp.exp(cum_end))):
            for h, x in enumerate(split(val)):
                name[j, h] = x
    units = [(j, h) for j in range(n_sub) for h in heads]
    l_ab = {u: jnp.where(strict, mm(ah[u], bh[u], _NT), 0.0) for u in units}
    l_ak = {u: jnp.where(strict, mm(ah[u], kh[u], _NT), 0.0) for u in units}
    m_rb = {u: jnp.where(lower, mm(rh[u], bh[u], _NT), 0.0) for u in units}
    m_rk = {u: jnp.where(lower, mm(rh[u], kh[u], _NT), 0.0) for u in units}
    lp = l_ab
    t_inv = {u: eye + l_ab[u] for u in units}
    for _ in range(n_double - 1):
        lp = {u: mm(lp[u], lp[u]) for u in units}
        t_inv = {u: t_inv[u] + mm(t_inv[u], lp[u]) for u in units}
    w = {u: mm(l_ak[u], vh[u]) for u in units}
    p1 = {u: mm(t_inv[u], ah[u]) for u in units}
    u0 = {u: mm(t_inv[u], w[u]) for u in units}
    y0 = {u: mm(m_rk[u], vh[u]) for u in units}
    h0 = {u: mm(vh[u], ke[u], _TN) for u in units}

    s = [s_ref[h] for h in heads]
    for j in range(n_sub):
        u = [mm(p1[j, h], s[h], _NT) + u0[j, h] for h in heads]
        y = [mm(rh[j, h], s[h], _NT) + mm(m_rb[j, h], u[h]) + y0[j, h] for h in heads]
        s = [s[h] * ga[j, h] + mm(u[h], be[j, h], _TN) + h0[j, h] for h in heads]
        for h in heads:
            yc = y[h] - jnp.mean(y[h], axis=-1, keepdims=True)
            y_ref[j * chunk:(j + 1) * chunk, h * DH_B:(h + 1) * DH_B] = (
                yc * lax.rsqrt(jnp.mean(yc * yc, axis=-1, keepdims=True) + LNX_EPS))
    for h in heads:
        s_ref[h] = s[h]
        sout_ref[0, h] = s[h]

    y_ref[...] = (y_ref[...] * lng_ref[...] + lnb_ref[...] + bonus_ref[...]) * g_ref[...]


def _rwkv_scan(streams, s0, lng, lnb, yb, row0, n_seq, seq_len, chunk, n_sub):
    n, w_b = streams[0].shape
    n_heads = w_b // DH_B
    blk_rows = chunk * n_sub
    n_steps = seq_len // blk_rows
    blk0 = row0 // blk_rows
    rows = pl.BlockSpec((blk_rows, w_b), lambda b, c: (blk0 + b * n_steps + c, 0))
    vec = pl.BlockSpec((1, w_b), lambda b, c: (0, 0))
    state = pl.BlockSpec((1, n_heads, DH_B, DH_B), lambda b, c: (b, 0, 0, 0))
    in_specs = [rows] * 8 + [state, vec, vec]
    args = list(streams) + [s0, lng, lnb]
    aliases = {}
    if yb is not None:
        in_specs.append(pl.BlockSpec(memory_space=pl.ANY))
        args.append(yb)
        aliases = {11: 0}
    kernel = functools.partial(_rwkv_scan_kernel, n_heads=n_heads, chunk=chunk)
    if yb is None:
        kernel = functools.partial(_scan_no_alias, kernel)
    return pl.pallas_call(
        kernel,
        grid=(n_seq, n_steps),
        in_specs=in_specs,
        out_specs=[rows, state],
        out_shape=[jax.ShapeDtypeStruct((n, w_b), F32),
                   jax.ShapeDtypeStruct((n_seq, n_heads, DH_B, DH_B), F32)],
        scratch_shapes=[pltpu.VMEM((n_heads, DH_B, DH_B), F32)],
        input_output_aliases=aliases,
        compiler_params=_cparams(("parallel", "arbitrary")),
        name="rwkv_scan",
    )(*args)


def _scan_no_alias(kernel, *refs):
    return kernel(*refs[:11], None, *refs[11:])


def _glu(pc, w_c):
    return pc[:, 0:w_c] * _sigmoid(pc[:, w_c:2 * w_c])


def _conv_norm_act(c, b_ref, lg_ref, lb_ref):
    c = c + b_ref[...]
    mu = jnp.mean(c, axis=-1, keepdims=True)
    cc = c - mu
    var = jnp.mean(cc * cc, axis=-1, keepdims=True)
    c = cc * lax.rsqrt(var + LN_EPS) * lg_ref[...] + lb_ref[...]
    return c * _sigmoid(c)


def _conv_prompt_kernel(pc_ref, halo_ref, w_ref, b_ref, lg_ref, lb_ref, y_ref, st_ref, buf_ref, sh_ref, *, sub):
    i = pl.program_id(0)
    bm, w_c = y_ref.shape
    halo = halo_ref.shape[0]
    n_buf = halo + bm
    buf_ref[0:halo, :] = jnp.where(i == 0, 0.0, _glu(halo_ref[...], w_c))
    buf_ref[halo:n_buf, :] = _glu(pc_ref[...], w_c)
    buf_ref[n_buf:n_buf + SUBLANES, :] = jnp.zeros((SUBLANES, w_c), F32)
    for r in range(1, SUBLANES):
        sh_ref[r - 1] = buf_ref[r:r + n_buf, :]
    off = halo - (CONV_K - 1)
    for s in range(bm // sub):
        acc = jnp.zeros((sub, w_c), F32)
        for j in range(CONV_K):
            r = (off + j) % SUBLANES
            q = s * sub + off + j - r
            src = buf_ref[q:q + sub, :] if r == 0 else sh_ref[r - 1, q:q + sub, :]
            acc = acc + w_ref[j:j + 1, :] * src
        y_ref[s * sub:(s + 1) * sub, :] = _conv_norm_act(acc, b_ref, lg_ref, lb_ref)
    st_ref[0] = buf_ref[n_buf - (CONV_K - 1):n_buf, :]


def _conv_prompt(p, col_blk, w, b, lg, lb, n_rows, t_prompt):
    bm = ATT_BLK
    halo = 32
    w_c = w.shape[1]
    vec = pl.BlockSpec((1, w_c), lambda i: (0, 0))
    return pl.pallas_call(
        functools.partial(_conv_prompt_kernel, sub=64),
        grid=(t_prompt // bm,),
        in_specs=[
            pl.BlockSpec((bm, 2 * w_c), lambda i: (i, col_blk)),
            pl.BlockSpec((halo, 2 * w_c), lambda i: (jnp.maximum(i * (bm // halo) - 1, 0), col_blk)),
            pl.BlockSpec((CONV_K, w_c), lambda i: (0, 0)),
            vec, vec, vec,
        ],
        out_specs=[
            pl.BlockSpec((bm, w_c), lambda i: (i, 0)),
            pl.BlockSpec((1, CONV_K - 1, w_c), lambda i: (0, 0, 0)),
        ],
        out_shape=[
            jax.ShapeDtypeStruct((n_rows, w_c), F32),
            jax.ShapeDtypeStruct((1, CONV_K - 1, w_c), F32),
        ],
        scratch_shapes=[pltpu.VMEM((halo + bm + SUBLANES, w_c), F32),
                        pltpu.VMEM((SUBLANES - 1, halo + bm, w_c), F32)],
        compiler_params=_cparams(("arbitrary",)),
        name="conv_prompt",
    )(p, p, w, b, lg, lb)


def _conv_sample_kernel(pc_ref, st0_ref, w_ref, b_ref, lg_ref, lb_ref, yc_in_ref, y_ref, st_ref, buf_ref):
    del yc_in_ref
    dec_seq, w_c = y_ref.shape
    hist = CONV_K - 1
    buf_ref[0:hist, :] = st0_ref[0]
    buf_ref[hist:hist + dec_seq, :] = _glu(pc_ref[...], w_c)
    acc = jnp.zeros((dec_seq, w_c), F32)
    for j in range(CONV_K):
        acc = acc + w_ref[j:j + 1, :] * buf_ref[j:j + dec_seq, :]
    y_ref[...] = _conv_norm_act(acc, b_ref, lg_ref, lb_ref)
    st_ref[0] = buf_ref[dec_seq:dec_seq + hist, :]


def _conv_sample(p, col_blk, st0, w, b, lg, lb, yc, t_prompt, dec_seq):
    n_seq = st0.shape[0]
    w_c = w.shape[1]
    hist = CONV_K - 1
    row0 = t_prompt // dec_seq
    vec = pl.BlockSpec((1, w_c), lambda s: (0, 0))
    return pl.pallas_call(
        _conv_sample_kernel,
        grid=(n_seq,),
        in_specs=[
            pl.BlockSpec((dec_seq, 2 * w_c), lambda s: (row0 + s, col_blk)),
            pl.BlockSpec((1, hist, w_c), lambda s: (s, 0, 0)),
            pl.BlockSpec((CONV_K, w_c), lambda s: (0, 0)),
            vec, vec, vec,
            pl.BlockSpec(memory_space=pl.ANY),
        ],
        out_specs=[
            pl.BlockSpec((dec_seq, w_c), lambda s: (row0 + s, 0)),
            pl.BlockSpec((1, hist, w_c), lambda s: (s, 0, 0)),
        ],
        out_shape=[
            jax.ShapeDtypeStruct(yc.shape, yc.dtype),
            jax.ShapeDtypeStruct((n_seq, hist, w_c), F32),
        ],
        scratch_shapes=[pltpu.VMEM((hist + dec_seq + 2, w_c), F32)],
        input_output_aliases={6: 0},
        compiler_params=_cparams(("parallel",)),
        name="conv_sample",
    )(p, st0, w, b, lg, lb, yc)


def _outproj_kernel(x_ref, ya_ref, yb_ref, yc_ref, wa_ref, wb_ref, wc_ref, o_ref):
    acc = x_ref[...] + _dot(ya_ref[...].astype(BF16), wa_ref[...])
    acc = acc + _dot(yb_ref[...].astype(BF16), wb_ref[...])
    o_ref[...] = acc + _dot(yc_ref[...].astype(BF16), wc_ref[...])


def _outproj(x, ya, yb, yc, wo, bm, bn):
    n, d = x.shape
    wa, wb, wc = ya.shape[1], yb.shape[1], yc.shape[1]
    row = lambda w: pl.BlockSpec((bm, w), lambda i, j: (i, 0))
    return pl.pallas_call(
        _outproj_kernel,
        grid=(n // bm, d // bn),
        in_specs=[
            pl.BlockSpec((bm, bn), lambda i, j: (i, j)),
            row(wa), row(wb), row(wc),
            pl.BlockSpec((wa, bn), lambda i, j: (0, j)),
            pl.BlockSpec((wb, bn), lambda i, j: (wa // wb, j)),
            pl.BlockSpec((wc, bn), lambda i, j: ((wa + wb) // wc, j)),
        ],
        out_specs=pl.BlockSpec((bm, bn), lambda i, j: (i, j)),
        out_shape=jax.ShapeDtypeStruct((n, d), F32),
        compiler_params=_cparams(("parallel", "arbitrary")),
        name="outproj",
    )(x, ya, yb, yc, wo, wo, wo)


def _ffn_kernel(x_ref, g_ref, wu_ref, wd_ref, o_ref, h_ref):
    f = pl.program_id(1)

    @pl.when(f == 0)
    def _():
        x = x_ref[...]
        ms = jnp.mean(x * x, axis=-1, keepdims=True)
        h_ref[...] = (x * lax.rsqrt(ms + RMS_EPS) * g_ref[...]).astype(BF16)
        o_ref[...] = x

    u = jnp.maximum(_dot(h_ref[...], wu_ref[...]), 0.0)
    o_ref[...] += _dot((u * u).astype(BF16), wd_ref[...])


def _ffn_split_kernel(x_ref, g_ref, wu_ref, wd_ref, o_ref, os_ref, h_ref, *, tail):
    _ffn_kernel(x_ref, g_ref, wu_ref, wd_ref, o_ref, h_ref)

    @pl.when(pl.program_id(1) == pl.num_programs(1) - 1)
    def _():
        os_ref[...] = o_ref[tail:tail + os_ref.shape[0], :]


def _ffn_split(x, g, wu, wd, bm, bf, t_prompt):
    n, d = x.shape
    d_ff = wu.shape[1]
    n_s = n - t_prompt
    tail = t_prompt - (n // bm - 1) * bm
    assert 0 <= tail and tail + n_s == bm and tail % SUBLANES == 0
    return pl.pallas_call(
        functools.partial(_ffn_split_kernel, tail=tail),
        grid=(n // bm, d_ff // bf),
        in_specs=[
            pl.BlockSpec((bm, d), lambda i, f: (i, 0)),
            pl.BlockSpec((1, d), lambda i, f: (0, 0)),
            pl.BlockSpec((d, bf), lambda i, f: (0, f)),
            pl.BlockSpec((bf, d), lambda i, f: (f, 0)),
        ],
        out_specs=[pl.BlockSpec((bm, d), lambda i, f: (i, 0)),
                   pl.BlockSpec((n_s, d), lambda i, f: (0, 0))],
        out_shape=[jax.ShapeDtypeStruct((t_prompt, d), F32), jax.ShapeDtypeStruct((n_s, d), F32)],
        scratch_shapes=[pltpu.VMEM((bm, d), BF16)],
        compiler_params=_cparams(("arbitrary", "arbitrary")),
        name="ffn_split",
    )(x, g, wu, wd)


def _ffn(x, g, wu, wd, bm, bf):
    n, d = x.shape
    d_ff = wu.shape[1]
    return pl.pallas_call(
        _ffn_kernel,
        grid=(n // bm, d_ff // bf),
        in_specs=[
            pl.BlockSpec((bm, d), lambda i, f: (i, 0)),
            pl.BlockSpec((1, d), lambda i, f: (0, 0)),
            pl.BlockSpec((d, bf), lambda i, f: (0, f)),
            pl.BlockSpec((bf, d), lambda i, f: (f, 0)),
        ],
        out_specs=pl.BlockSpec((bm, d), lambda i, f: (i, 0)),
        out_shape=jax.ShapeDtypeStruct((n, d), F32),
        scratch_shapes=[pltpu.VMEM((bm, d), BF16)],
        compiler_params=_cparams(("parallel", "arbitrary")),
        name="ffn",
    )(x, g, wu, wd)


def _row_tile(n, target):
    best = SUBLANES
    for t in range(SUBLANES, target + 1, SUBLANES):
        if n % t == 0:
            best = t
    return best


def _lambda_init(l):
    return 0.8 - 0.6 * math.exp(-0.3 * l)


def _pad_rows(w, start, total):
    return jnp.zeros((total, w.shape[1]), w.dtype).at[start:start + w.shape[0]].set(w)


def kernel(x_prompt, x_sample, cache_k, cache_v, state_rwkv, state_shift, state_conv, page_table, rel_bias, norm_mix_g, w_in, q_norm_g, k_norm_g, lambda_q1, lambda_k1, lambda_q2, lambda_k2, subln_g, rwkv_mu, rwkv_w0, rwkv_w2, rwkv_a0, rwkv_a2, rwkv_g2, rwkv_k_k, rwkv_k_a, rwkv_r_k, rwkv_lnx_g, rwkv_lnx_b, conv_dw_w, conv_dw_b, conv_ln_g, conv_ln_b, w_out, norm_ffn_g, w_up, w_down):
    depth = w_in.shape[0]
    b_p, t_p, d = x_prompt.shape
    n_seq, dec_seq, _ = x_sample.shape
    assert b_p == 1 and dec_seq == SUBLANES and t_p % ATT_BLK == 0 and (n_seq * dec_seq) % ATT_BLK == 0
    w_a, w_b, w_c = d // 2, d // 4, d // 4
    n_heads = w_a // DV_A
    h_b = w_b // DH_B
    shift_w = state_shift.shape[-1]
    assert shift_w == 3 * w_b + DECAY_LORA + AAA_LORA + GATE_LORA and shift_w <= 3 * w_b + LORA_W
    shift_pad = 3 * w_b + LORA_W
    page = cache_k.shape[2]
    assert page == LANES and page >= MAX_DISTANCE and ATT_BLK >= MAX_DISTANCE
    n = t_p + n_seq * dec_seq
    bm = _row_tile(n, 1024)

    x = jnp.concatenate([x_prompt.reshape(t_p, d), x_sample.reshape(n_seq * dec_seq, d)], axis=0)
    bias_near, bias_last, bias_new = _bias_tiles(rel_bias, n_heads, ATT_BLK, page, dec_seq)

    a_end, b_end = 3 * w_a, 3 * w_a + shift_w
    col_c = 3 * w_a // (2 * w_c)
    col_b = (3 * w_a + 2 * w_c) // shift_pad
    assert col_c * 2 * w_c == 3 * w_a and col_b * shift_pad == 3 * w_a + 2 * w_c

    sp_l, ss_l, shp_l, shs_l, cvp_l, cvs_l = ([] for _ in range(6))
    kv_out = None
    for l in range(depth):
        w_in_p = _win_prep(w_in, l, a_end, b_end, shift_pad - shift_w)
        p = _inproj(x, norm_mix_g[l][None], w_in_p, bm, 2048)

        lam_init = _lambda_init(l)
        qg = jnp.tile(q_norm_g[l], 2)[None]
        kg = jnp.tile(k_norm_g[l], 2)[None]
        lamv = jnp.stack([lambda_q1[l], lambda_k1[l], lambda_q2[l], lambda_k2[l]])
        sg = subln_g[l][None]
        qt, kb, vt, *kv_out = _qkv_prep(p, qg, kg, n_heads, l, depth, t_p, kv_out)
        ya = _attn_prompt(qt, kb, vt, bias_near, lamv, subln_g[l][:, None], jax.ShapeDtypeStruct((n, w_a), F32),
                          t_p, n_heads, lam_init)
        ya = _attn_sample(page_table, p, qg, kv_out[2], cache_k, cache_v, l, bias_last, bias_new, lamv, sg, ya,
                          t_p, dec_seq, n_heads, lam_init)

        pad = ((0, 0), (0, shift_pad - shift_w))
        ov = jnp.pad(jnp.repeat(state_shift[l], dec_seq, axis=0), pad)
        lo = 0
        w2p = _pad_rows(rwkv_w2[l], lo, LORA_W).astype(BF16)
        a2p = _pad_rows(rwkv_a2[l], lo + DECAY_LORA, LORA_W).astype(BF16)
        g2p = _pad_rows(rwkv_g2[l], lo + DECAY_LORA + AAA_LORA, LORA_W).astype(BF16)
        streams = _rwkv_prep(p, col_b, ov, jnp.pad(rwkv_mu[l][None], pad), rwkv_w0[l][None], rwkv_a0[l][None],
                             rwkv_k_k[l][None], rwkv_k_a[l][None], rwkv_r_k[l].reshape(1, w_b),
                             w2p, a2p, g2p, t_p, dec_seq)
        lng, lnb = rwkv_lnx_g[l][None], rwkv_lnx_b[l][None]
        yb, s_p = _rwkv_scan(streams, jnp.zeros((b_p, h_b, DH_B, DH_B), F32), lng, lnb, None,
                             0, b_p, t_p, RW_CHUNK, RW_CHUNKS_PER_STEP)
        yb, s_s = _rwkv_scan(streams, state_rwkv[l], lng, lnb, yb, t_p, n_seq, dec_seq, dec_seq, 1)
        sp_l.append(s_p)
        ss_l.append(s_s)
        c0 = col_b * shift_pad
        shp_l.append(p[t_p - 1:t_p, c0:c0 + shift_w])
        shs_l.append(p[t_p + dec_seq - 1::dec_seq, c0:c0 + shift_w])

        cw, cb = conv_dw_w[l], conv_dw_b[l][None]
        clg, clb = conv_ln_g[l][None], conv_ln_b[l][None]
        yc, cv_p = _conv_prompt(p, col_c, cw, cb, clg, clb, n, t_p)
        yc, cv_s = _conv_sample(p, col_c, state_conv[l], cw, cb, clg, clb, yc, t_p, dec_seq)
        cvp_l.append(cv_p)
        cvs_l.append(cv_s)

        x = _outproj(x, ya, yb, yc, _cast_bf16(w_out, l), _row_tile(n, 512), d)
        ffn_args = (x, norm_ffn_g[l][None], _cast_bf16(w_up, l), _cast_bf16(w_down, l), bm, 1024)
        if l + 1 < depth:
            x = _ffn(*ffn_args)
        else:
            x_p, x_s = _ffn_split(*ffn_args, t_p)

    y_prompt = x_p.reshape(b_p, t_p, d)
    y_sample = x_s.reshape(n_seq, dec_seq, d)
    k_p, v_p, k_s, v_s = kv_out
    return (y_prompt, y_sample,
            k_p.reshape(depth, b_p, t_p, n_heads, DV_A), v_p.reshape(depth, b_p, t_p, n_heads, DV_A),
            k_s.reshape(depth, n_seq, dec_seq, n_heads, DV_A), v_s.reshape(depth, n_seq, dec_seq, n_heads, DV_A),
            jnp.stack(sp_l), jnp.stack(ss_l), jnp.stack(shp_l), jnp.stack(shs_l),
            jnp.stack(cvp_l), jnp.stack(cvs_l))
```
